```python
import math
import jax, jax.numpy as jnp
from jax import lax
import numpy as np

D_MODEL = 1024
BATCH = 2
SEQ = 8192
DEPTH = 2
DEC_BATCH = 16
DEC_SEQ = 64
PAST_LEN = 2048

CHUNK = 64
N_A_LAYERS = DEPTH // 2
N_B_LAYERS = DEPTH - N_A_LAYERS
H_A = 8
DA = 64
DV_A = 2 * DA
ROPE_THETA = 10000.0
Q_BLOCK = 128
H_B = 16
DB = D_MODEL // H_B
BAND_CHUNKS = 8
BAND_PAST = BAND_CHUNKS * CHUNK
REL_MAX = 128
N_REL = (CHUNK - 1) + REL_MAX + 1
N_EXPERTS = 32
TOP_K = 4
D_FF = D_MODEL
SWIGLU_LIMIT = 7.0
SWIGLU_ALPHA = 1.702
MOE_GROUP = 128
NORM_EPS = 1e-6
NEG_INF = -1e30

kernel_name = "yoco_diffattn_chunkband_moe_stream_step"

f32 = jnp.float32


def rms_norm(x, g):
    xf = x.astype(f32)
    y = xf * lax.rsqrt(jnp.mean(xf * xf, axis=-1, keepdims=True) + NORM_EPS)
    return (y * g.astype(f32)).astype(x.dtype)


def modulate(x, g, shift, scale):
    return rms_norm(x, g) * (1 + scale[:, None, :]) + shift[:, None, :]


def rope(x, pos):
    half = x.shape[-1] // 2
    inv = ROPE_THETA ** (-jnp.arange(half, dtype=f32) / half)
    ang = pos.astype(f32)[:, None] * inv[None, :]
    shp = (1, pos.shape[0]) + (1,) * (x.ndim - 3) + (half,)
    cos = jnp.cos(ang).reshape(shp)
    sin = jnp.sin(ang).reshape(shp)
    xf = x.astype(f32)
    x1, x2 = xf[..., :half], xf[..., half:]
    return jnp.concatenate([x1 * cos - x2 * sin, x2 * cos + x1 * sin], axis=-1).astype(x.dtype)


def diff_qkv(xn, pos, w_qkv, q_norm, k_norm):
    B, S, _ = xn.shape
    q, k, v = jnp.split(xn @ w_qkv, 3, axis=-1)
    q = rope(rms_norm(q.reshape(B, S, 2, H_A, DA), q_norm), pos)
    k = rope(rms_norm(k.reshape(B, S, 2, H_A, DA), k_norm), pos)
    v = v.reshape(B, S, H_A, DV_A)
    return q, k, v


def diff_lambda(lam_params, layer_idx):
    lam_init = 0.8 - 0.6 * math.exp(-0.3 * layer_idx)
    lp = lam_params.astype(f32)
    lam = jnp.exp(jnp.sum(lp[0] * lp[1])) - jnp.exp(jnp.sum(lp[2] * lp[3])) + lam_init
    return lam, lam_init


def diff_attn_block(q, qpos, k, v, kpos, lam):
    s = jnp.einsum('bqmhd,bkmhd->bmhqk', q, k, preferred_element_type=f32) * (DA ** -0.5)
    visible = (kpos[None, :] // CHUNK) <= (qpos[:, None] // CHUNK)
    s = jnp.where(visible, s, NEG_INF)
    p = jax.nn.softmax(s, axis=-1)
    w = p[:, 0] - lam * p[:, 1]
    return jnp.einsum('bhqk,bkhd->bqhd', w.astype(v.dtype), v)


def diff_attn_prompt(q, k, v, lam):
    B, S = q.shape[:2]
    nb = S // Q_BLOCK
    kpos = jnp.arange(S)
    qb = jnp.moveaxis(q.reshape(B, nb, Q_BLOCK, 2, H_A, DA), 1, 0)

    def one(args):
        qi, bi = args
        qpos = bi * Q_BLOCK + jnp.arange(Q_BLOCK)
        return diff_attn_block(qi, qpos, k, v, kpos, lam)

    o = lax.map(one, (qb, jnp.arange(nb)))
    return jnp.moveaxis(o, 0, 1).reshape(B, S, H_A, DV_A)


def diff_out(o, subln, lam_init, w_o):
    B, S = o.shape[:2]
    o = rms_norm(o, subln) * (1.0 - lam_init)
    return o.reshape(B, S, H_A * DV_A) @ w_o


def shared_kv(h, shift, scale, norm_kv, w_kv, k_norm):
    B, S, _ = h.shape
    k, v = jnp.split(modulate(h, norm_kv, shift, scale) @ w_kv, 2, axis=-1)
    k = rms_norm(k.reshape(B, S, H_B, DB), k_norm)
    return k, v.reshape(B, S, H_B, DB)


def rel_bias(table, n_q, n_past, n_k):
    dist = jnp.arange(n_q)[:, None] + n_past - jnp.arange(n_k)[None, :]
    idx = jnp.clip(dist, -(CHUNK - 1), REL_MAX) + (CHUNK - 1)
    return table.astype(f32)[:, idx]


def band_attn_block(q, kb, vb, bias, valid):
    s = jnp.einsum('bqhd,bkhd->bhqk', q, kb, preferred_element_type=f32) * (DB ** -0.5) + bias[None]
    s = jnp.where(valid[None, None, None, :], s, NEG_INF)
    p = jax.nn.softmax(s, axis=-1)
    return jnp.einsum('bhqk,bkhd->bqhd', p.astype(vb.dtype), vb)


def band_attn_prompt(q, k_pad, v_pad, table):
    B, S = q.shape[:2]
    nc = S // CHUNK
    band = BAND_PAST + CHUNK
    valid_pad = jnp.arange(S + BAND_PAST) >= BAND_PAST
    bias = rel_bias(table, CHUNK, BAND_PAST, band)
    qc = jnp.moveaxis(q.reshape(B, nc, CHUNK, H_B, DB), 1, 0)

    def one(args):
        qi, ci = args
        st = ci * CHUNK
        kb = lax.dynamic_slice_in_dim(k_pad, st, band, axis=1)
        vb = lax.dynamic_slice_in_dim(v_pad, st, band, axis=1)
        val = lax.dynamic_slice_in_dim(valid_pad, st, band)
        return band_attn_block(qi, kb, vb, bias, val)

    o = lax.map(one, (qc, jnp.arange(nc)))
    return jnp.moveaxis(o, 0, 1).reshape(B, S, H_B, DB)


def clamped_swiglu(h):
    x_glu, x_lin = h[..., ::2], h[..., 1::2]
    x_glu = jnp.minimum(x_glu, SWIGLU_LIMIT)
    x_lin = jnp.clip(x_lin, -SWIGLU_LIMIT, SWIGLU_LIMIT)
    return x_glu * jax.nn.sigmoid(SWIGLU_ALPHA * x_glu) * (x_lin + 1)


def moe(x2, w_router, b_router, w_gu, b_gu, w_down, b_down):
    T, D = x2.shape
    logits = (x2 @ w_router + b_router).astype(f32)
    top_val, top_idx = lax.top_k(logits, TOP_K)
    gates = jax.nn.softmax(top_val, axis=-1)
    A = T * TOP_K
    e_flat = top_idx.reshape(A)
    t_flat = jnp.repeat(jnp.arange(T, dtype=jnp.int32), TOP_K)
    g_flat = gates.reshape(A)
    order = jnp.argsort(e_flat)
    es, ts, gs = e_flat[order], t_flat[order], g_flat[order]
    counts = jnp.bincount(e_flat, length=N_EXPERTS)
    padded = (counts + MOE_GROUP - 1) // MOE_GROUP * MOE_GROUP
    ends = jnp.cumsum(padded)
    pstarts = ends - padded
    starts = jnp.cumsum(counts) - counts
    slot = pstarts[es] + (jnp.arange(A) - starts[es])
    n_blocks = -(-(A + N_EXPERTS * (MOE_GROUP - 1)) // MOE_GROUP)
    block_expert = jnp.minimum(
        jnp.searchsorted(ends, jnp.arange(n_blocks) * MOE_GROUP, side='right'), N_EXPERTS - 1)
    x_buf = jnp.zeros((n_blocks * MOE_GROUP, D), x2.dtype).at[slot].set(x2[ts])

    def expert_block(args):
        xb, e = args
        hb = clamped_swiglu(xb @ w_gu[e] + b_gu[e])
        return hb @ w_down[e] + b_down[e]

    y_buf = lax.map(expert_block, (x_buf.reshape(n_blocks, MOE_GROUP, D), block_expert)).reshape(-1, D)
    return jax.ops.segment_sum(y_buf[slot] * gs[:, None].astype(x2.dtype), ts, num_segments=T)


def run_group(x, c, p, band_rows, past):
    B, S, D = x.shape
    past_len = 0 if past is None else past[0].shape[2]
    pos = past_len + jnp.arange(S)
    cs = jax.nn.silu(c.astype(f32)).astype(x.dtype)
    h = x
    a_k_rows, a_v_rows = [], []
    kv_b = None
    for l in range(DEPTH):
        mod = cs @ p['w_mod'][l] + p['b_mod'][l]
        sh1, sc1, g1, sh2, sc2, g2 = jnp.split(mod, 6, axis=-1)
        xn = modulate(h, p['norm_attn'][l], sh1, sc1)
        if l < N_A_LAYERS:
            i = l
            q, k, v = diff_qkv(xn, pos, p['w_qkv_a'][i], p['q_norm_a'][i], p['k_norm_a'][i])
            lam, lam_init = diff_lambda(p['lambda_a'][i], l)
            if past is None:
                o = diff_attn_prompt(q, k, v, lam)
            else:
                ck = past[0][i].reshape(B, past_len, 2, H_A, DA)
                kk = jnp.concatenate([ck, k], axis=1)
                vv = jnp.concatenate([past[1][i], v], axis=1)
                o = diff_attn_block(q, pos, kk, vv, jnp.arange(past_len + S), lam)
            mix = diff_out(o, p['subln_a'][i], lam_init, p['w_o_a'][i])
            a_k_rows.append(k.reshape(B, S, 2 * H_A, DA))
            a_v_rows.append(v)
        else:
            i = l - N_A_LAYERS
            if kv_b is None:
                kv_mod = cs @ p['w_mod_kv'] + p['b_mod_kv']
                shk, sck = jnp.split(kv_mod, 2, axis=-1)
                kb_new, vb_new = shared_kv(h, shk, sck, p['norm_kv'], p['w_kv_b'], p['k_norm_b'])
                if past is None:
                    padw = ((0, 0), (BAND_PAST, 0), (0, 0), (0, 0))
                    kv_b = (jnp.pad(kb_new, padw), jnp.pad(vb_new, padw))
                else:
                    kv_b = (jnp.concatenate([past[2], kb_new], axis=1),
                            jnp.concatenate([past[3], vb_new], axis=1))
            qb = rms_norm((xn @ p['w_q_b'][i]).reshape(B, S, H_B, DB), p['q_norm_b'][i])
            if past is None:
                o = band_attn_prompt(qb, kv_b[0], kv_b[1], p['rel_bias_b'][i])
            else:
                R = past[2].shape[1]
                bias = rel_bias(p['rel_bias_b'][i], S, R, R + S)
                o = band_attn_block(qb, kv_b[0], kv_b[1], bias, jnp.ones((R + S,), dtype=bool))
            mix = o.reshape(B, S, D) @ p['w_o_b'][i]
        h = h + g1[:, None, :] * mix
        xn2 = modulate(h, p['norm_ffn'][l], sh2, sc2)
        ff = moe(xn2.reshape(B * S, D), p['w_router'][l], p['b_router'][l], p['w_gu'][l],
                 p['b_gu'][l], p['w_down'][l], p['b_down'][l]).reshape(B, S, D)
        h = h + g2[:, None, :] * ff
    new_b_k = kv_b[0][:, -band_rows:]
    new_b_v = kv_b[1][:, -band_rows:]
    return h, jnp.stack(a_k_rows), jnp.stack(a_v_rows), new_b_k, new_b_v


def setup_inputs(seed: int = 0) -> dict:
    key = jax.random.key(seed)
    ks = iter(jax.random.split(key, 40))

    def nrm(shape, scale):
        return scale * jax.random.normal(next(ks), shape, jnp.float32)

    D = D_MODEL
    R = min(BAND_PAST, PAST_LEN)
    s_in = D ** -0.5
    return {
        'x_prompt': nrm((BATCH, SEQ, D), 1.0),
        'x_sample': nrm((DEC_BATCH, DEC_SEQ, D), 1.0),
        'c_prompt': nrm((BATCH, D), 1.0),
        'c_sample': nrm((DEC_BATCH, D), 1.0),
        'cache_a_k': nrm((N_A_LAYERS, DEC_BATCH, PAST_LEN, 2 * H_A, DA), 1.0),
        'cache_a_v': nrm((N_A_LAYERS, DEC_BATCH, PAST_LEN, H_A, DV_A), 1.0),
        'cache_b_k': nrm((DEC_BATCH, R, H_B, DB), 1.0),
        'cache_b_v': nrm((DEC_BATCH, R, H_B, DB), 1.0),
        'w_mod': nrm((DEPTH, D, 6 * D), 0.5 * s_in),
        'b_mod': nrm((DEPTH, 6 * D), 0.02),
        'norm_attn': 1.0 + nrm((DEPTH, D), 0.02),
        'norm_ffn': 1.0 + nrm((DEPTH, D), 0.02),
        'w_qkv_a': nrm((N_A_LAYERS, D, 3 * D), s_in),
        'q_norm_a': 1.0 + nrm((N_A_LAYERS, DA), 0.02),
        'k_norm_a': 1.0 + nrm((N_A_LAYERS, DA), 0.02),
        'lambda_a': nrm((N_A_LAYERS, 4, DA), 0.1),
        'subln_a': 1.0 + nrm((N_A_LAYERS, DV_A), 0.02),
        'w_o_a': nrm((N_A_LAYERS, D, D), s_in),
        'w_mod_kv': nrm((D, 2 * D), 0.5 * s_in),
        'b_mod_kv': nrm((2 * D,), 0.02),
        'norm_kv': 1.0 + nrm((D,), 0.02),
        'w_kv_b': nrm((D, 2 * D), s_in),
        'k_norm_b': 1.0 + nrm((DB,), 0.02),
        'w_q_b': nrm((N_B_LAYERS, D, D), s_in),
        'q_norm_b': 1.0 + nrm((N_B_LAYERS, DB), 0.02),
        'rel_bias_b': nrm((N_B_LAYERS, H_B, N_REL), 0.5),
        'w_o_b': nrm((N_B_LAYERS, D, D), s_in),
        'w_router': nrm((DEPTH, D, N_EXPERTS), s_in),
        'b_router': nrm((DEPTH, N_EXPERTS), 0.01),
        'w_gu': nrm((DEPTH, N_EXPERTS, D, 2 * D_FF), s_in),
        'b_gu': nrm((DEPTH, N_EXPERTS, 2 * D_FF), 0.02),
        'w_down': nrm((DEPTH, N_EXPERTS, D_FF, D), D_FF ** -0.5),
        'b_down': nrm((DEPTH, N_EXPERTS, D), 0.02),
    }


def reference(x_prompt, x_sample, c_prompt, c_sample, cache_a_k, cache_a_v, cache_b_k, cache_b_v,
              w_mod, b_mod, norm_attn, norm_ffn, w_qkv_a, q_norm_a, k_norm_a, lambda_a, subln_a, w_o_a,
              w_mod_kv, b_mod_kv, norm_kv, w_kv_b, k_norm_b, w_q_b, q_norm_b, rel_bias_b, w_o_b,
              w_router, b_router, w_gu, b_gu, w_down, b_down):
    p = {
        'w_mod': w_mod, 'b_mod': b_mod, 'norm_attn': norm_attn, 'norm_ffn': norm_ffn,
        'w_qkv_a': w_qkv_a, 'q_norm_a': q_norm_a, 'k_norm_a': k_norm_a, 'lambda_a': lambda_a,
        'subln_a': subln_a, 'w_o_a': w_o_a,
        'w_mod_kv': w_mod_kv, 'b_mod_kv': b_mod_kv, 'norm_kv': norm_kv, 'w_kv_b': w_kv_b,
        'k_norm_b': k_norm_b, 'w_q_b': w_q_b, 'q_norm_b': q_norm_b, 'rel_bias_b': rel_bias_b,
        'w_o_b': w_o_b,
        'w_router': w_router, 'b_router': b_router, 'w_gu': w_gu, 'b_gu': b_gu,
        'w_down': w_down, 'b_down': b_down,
    }
    band_rows = cache_b_k.shape[1]
    y_prompt, ak_p, av_p, bk_p, bv_p = run_group(x_prompt, c_prompt, p, band_rows, None)
    y_sample, ak_s, av_s, bk_s, bv_s = run_group(
        x_sample, c_sample, p, band_rows, (cache_a_k, cache_a_v, cache_b_k, cache_b_v))
    return (y_prompt, y_sample, ak_p, av_p, bk_p, bv_p, ak_s, av_s, bk_s, bv_s)
```

```python
import functools
import math

import jax
import jax.numpy as jnp
import numpy as np
from jax import lax
from jax.experimental import pallas as pl
from jax.experimental.pallas import tpu as pltpu

F32 = jnp.float32
BF16 = jnp.bfloat16
I32 = jnp.int32

D_MODEL = 1024
CHUNK = 64
H_A = 8
DA = 64
DV_A = 2 * DA
ROPE_THETA = 10000.0
H_B = 16
DB = D_MODEL // H_B
BAND_PAST = 8 * CHUNK
REL_MAX = 128
N_EXPERTS = 32
TOP_K = 4
D_FF = D_MODEL
SWIGLU_LIMIT = 7.0
SWIGLU_ALPHA = 1.702
NORM_EPS = 1e-6
NEG_INF = -1e30

LANES = 128
VMEM_LIMIT = 56 * 2**20


def _params(sem):
    return pltpu.CompilerParams(dimension_semantics=sem, vmem_limit_bytes=VMEM_LIMIT)


def _modnorm(x, gain, shift, scale):
    ms = jnp.mean(x * x, axis=-1, keepdims=True)
    y = x * lax.rsqrt(ms + NORM_EPS) * gain
    return y * (1.0 + scale) + shift


def _group_ones(width):
    r = lax.broadcasted_iota(I32, (LANES, LANES), 0) // width
    c = lax.broadcasted_iota(I32, (LANES, LANES), 1) // width
    return jnp.where(r == c, 1.0, 0.0).astype(BF16)


def _headnorm64(z, grp, gain):
    zz = z * z
    hi = zz.astype(BF16)
    lo = (zz - hi.astype(F32)).astype(BF16)
    ss = (jnp.dot(hi, grp, preferred_element_type=F32)
          + jnp.dot(lo, grp, preferred_element_type=F32))
    return z * lax.rsqrt(ss * (1.0 / DA) + NORM_EPS) * gain


def _rope(z, cos, sin_signed, first_half):
    fwd = pltpu.roll(z, LANES - DA // 2, axis=1)
    bwd = pltpu.roll(z, DA // 2, axis=1)
    return z * cos + jnp.where(first_half, fwd, bwd) * sin_signed


def _mod_kernel(c_ref, w_ref, b_ref, o_ref):
    c = c_ref[...]
    cs = (c * jax.nn.sigmoid(c)).astype(BF16)
    o_ref[0] = jnp.dot(cs, w_ref[0].astype(BF16), preferred_element_type=F32) + b_ref[0]


def _mod_call(c_all, w, b):
    L, D, N = w.shape
    R = c_all.shape[0]
    tn = 1024
    return pl.pallas_call(
        _mod_kernel,
        grid=(L, N // tn),
        in_specs=[pl.BlockSpec((R, D), lambda l, j: (0, 0)),
                  pl.BlockSpec((1, D, tn), lambda l, j: (l, 0, j)),
                  pl.BlockSpec((1, 1, tn), lambda l, j: (l, 0, j))],
        out_specs=pl.BlockSpec((1, R, tn), lambda l, j: (l, 0, j)),
        out_shape=jax.ShapeDtypeStruct((L, R, N), F32),
        compiler_params=_params(("arbitrary", "arbitrary")),
        name="adaln_mod",
    )(c_all, w, b)


def _proj_kernel(*refs, sections, pad):
    x_ref, g_ref, sh_ref, sc_ref, w_ref, gains_ref, cos_ref, sin_ref = refs[:8]
    out_refs = refs[8:]
    tm = x_ref.shape[0]

    def compute():
        xn = _modnorm(x_ref[...], g_ref[...], sh_ref[0], sc_ref[0])
        y = jnp.dot(xn.astype(BF16), w_ref[...], preferred_element_type=F32)
        grp = _group_ones(DA)
        lane = lax.broadcasted_iota(I32, (tm, LANES), 1)
        first_half = (lane % DA) < (DA // 2)
        oi = 0
        for si, (norm, rope, scale, want_f32, want_bf16) in enumerate(sections):
            outs = []
            if want_f32:
                outs.append(out_refs[oi]); oi += 1
            if want_bf16:
                outs.append(out_refs[oi]); oi += 1
            for s in range(D_MODEL // LANES):
                col = si * D_MODEL + s * LANES
                z = y[:, col:col + LANES]
                if norm:
                    z = _headnorm64(z, grp, gains_ref[si:si + 1, :])
                if rope:
                    z = _rope(z, cos_ref[...], sin_ref[...], first_half)
                if scale != 1.0:
                    z = z * scale
                for o in outs:
                    if pad:
                        o[0, :, s * LANES:(s + 1) * LANES] = z.astype(o.dtype)
                    else:
                        o[:, s * LANES:(s + 1) * LANES] = z.astype(o.dtype)

    if pad:
        i = pl.program_id(1)

        @pl.when(i == 0)
        def _():
            for o in out_refs:
                o[...] = jnp.zeros(o.shape, o.dtype)

        pl.when(i > 0)(compute)
    else:
        compute()


def _proj_call(x, gain, shift, scale, w, gains, cos, sin, *, rows_per_batch, tm, sections,
               pad=False, name):
    T, D = x.shape
    N = w.shape[1]
    nb = T // rows_per_batch
    ni = rows_per_batch // tm
    out_shapes, out_specs = [], []
    for (norm, rope, scl, want_f32, want_bf16) in sections:
        for want, dt in ((want_f32, F32), (want_bf16, BF16)):
            if not want:
                continue
            if pad:
                out_shapes.append(jax.ShapeDtypeStruct((nb, BAND_PAST + rows_per_batch, D_MODEL), dt))
                out_specs.append(pl.BlockSpec((1, tm, D_MODEL), lambda b, i: (b, i, 0)))
            else:
                out_shapes.append(jax.ShapeDtypeStruct((T, D_MODEL), dt))
                out_specs.append(pl.BlockSpec((tm, D_MODEL), lambda b, i: (b * ni + i, 0)))
    if pad:
        assert tm == BAND_PAST
        grid = (nb, ni + 1)
        xmap = lambda b, i: (b * ni + jnp.maximum(i - 1, 0), 0)
        cmap = lambda b, i: (jnp.maximum(i - 1, 0), 0)
    else:
        grid = (nb, ni)
        xmap = lambda b, i: (b * ni + i, 0)
        cmap = lambda b, i: (i, 0)
    kern = functools.partial(_proj_kernel, sections=sections, pad=pad)
    return pl.pallas_call(
        kern,
        grid=grid,
        in_specs=[pl.BlockSpec((tm, D), xmap),
                  pl.BlockSpec((1, D), lambda b, i: (0, 0)),
                  pl.BlockSpec((1, 1, D), lambda b, i: (b, 0, 0)),
                  pl.BlockSpec((1, 1, D), lambda b, i: (b, 0, 0)),
                  pl.BlockSpec((D, N), lambda b, i: (0, 0)),
                  pl.BlockSpec(gains.shape, lambda b, i: (0, 0)),
                  pl.BlockSpec((tm, LANES), cmap),
                  pl.BlockSpec((tm, LANES), cmap)],
        out_specs=out_specs,
        out_shape=out_shapes,
        compiler_params=_params(("arbitrary", "arbitrary")),
        name=name,
    )(x, gain, shift, scale, w, gains, cos, sin)


def _diff_lambda(lam_ref, lam_init):
    lp = lam_ref[...]
    s1 = jnp.sum(lp[0:1, :] * lp[1:2, :], axis=-1, keepdims=True)
    s2 = jnp.sum(lp[2:3, :] * lp[3:4, :], axis=-1, keepdims=True)
    return jnp.exp(s1) - jnp.exp(s2) + lam_init


def _softmax_update(s, v, m_ref, l_ref, acc_ref, c):
    m_old = m_ref[c]
    m_new = jnp.maximum(m_old, jnp.max(s, axis=-1, keepdims=True))
    alpha = jnp.exp(m_old - m_new)
    p = jnp.exp(s - m_new)
    l_ref[c] = alpha * l_ref[c] + jnp.sum(p, axis=-1, keepdims=True)
    acc_ref[c] = alpha * acc_ref[c] + jnp.dot(p.astype(BF16), v, preferred_element_type=F32)
    m_ref[c] = m_new


def _nt_dot(a, b):
    return lax.dot_general(a, b, (((1,), (1,)), ((), ())), preferred_element_type=F32)


def _flash_kernel(qi_tab, ki_tab, q0_ref, q1_ref, k0_ref, k1_ref, v_ref, lam_ref, o_ref,
                  m_ref, l_ref, acc_ref, *, lam_init):
    step = pl.program_id(2)
    qi = qi_tab[step]
    ki = ki_tab[step]
    tq = q0_ref.shape[0]
    tk = k0_ref.shape[0]

    @pl.when(ki == 0)
    def _():
        m_ref[...] = jnp.full(m_ref.shape, NEG_INF, F32)
        l_ref[...] = jnp.zeros(l_ref.shape, F32)
        acc_ref[...] = jnp.zeros(acc_ref.shape, F32)

    def update(masked):
        lane = lax.broadcasted_iota(I32, (tq, LANES), 1)
        if masked:
            rq = lax.broadcasted_iota(I32, (tq, tk), 0) // CHUNK
            ck = lax.broadcasted_iota(I32, (tq, tk), 1) // CHUNK
            visible = ck <= rq
        for mp, (q_ref, k_ref) in enumerate(((q0_ref, k0_ref), (q1_ref, k1_ref))):
            q = q_ref[...]
            k = k_ref[...]
            for a in range(2):
                qa = jnp.where((lane >= DA) == bool(a), q, jnp.zeros_like(q))
                s = _nt_dot(qa, k)
                if masked:
                    s = jnp.where(visible, s, NEG_INF)
                _softmax_update(s, v_ref[:, a * DV_A:(a + 1) * DV_A], m_ref, l_ref, acc_ref,
                                mp * 2 + a)

    pl.when(ki < qi)(lambda: update(False))
    pl.when(ki == qi)(lambda: update(True))

    @pl.when(ki == qi)
    def _():
        lam = _diff_lambda(lam_ref, lam_init)
        for a in range(2):
            o0 = acc_ref[a] / l_ref[a]
            o1 = acc_ref[2 + a] / l_ref[2 + a]
            o_ref[:, a * DV_A:(a + 1) * DV_A] = o0 - lam * o1


def _flash_call(q, kb, vb, lam_params, *, nb, seq, lam_init, tq=512):
    T = q.shape[0]
    n = seq // tq
    pairs = [(qi, ki) for qi in range(n) for ki in range(qi + 1)]
    qi_tab = jnp.asarray(np.array([p[0] for p in pairs], np.int32))
    ki_tab = jnp.asarray(np.array([p[1] for p in pairs], np.int32))
    npair = H_A // 2
    grid_spec = pltpu.PrefetchScalarGridSpec(
        num_scalar_prefetch=2,
        grid=(nb, npair, len(pairs)),
        in_specs=[
            pl.BlockSpec((tq, LANES), lambda b, j, s, qt, kt: (b * n + qt[s], j)),
            pl.BlockSpec((tq, LANES), lambda b, j, s, qt, kt: (b * n + qt[s], npair + j)),
            pl.BlockSpec((tq, LANES), lambda b, j, s, qt, kt: (b * n + kt[s], j)),
            pl.BlockSpec((tq, LANES), lambda b, j, s, qt, kt: (b * n + kt[s], npair + j)),
            pl.BlockSpec((tq, 2 * DV_A), lambda b, j, s, qt, kt: (b * n + kt[s], j)),
            pl.BlockSpec((4, DA), lambda b, j, s, qt, kt: (0, 0)),
        ],
        out_specs=pl.BlockSpec((tq, 2 * DV_A), lambda b, j, s, qt, kt: (b * n + qt[s], j)),
        scratch_shapes=[pltpu.VMEM((4, tq, 1), F32), pltpu.VMEM((4, tq, 1), F32),
                        pltpu.VMEM((4, tq, DV_A), F32)],
    )
    return pl.pallas_call(
        functools.partial(_flash_kernel, lam_init=lam_init),
        grid_spec=grid_spec,
        out_shape=jax.ShapeDtypeStruct((T, D_MODEL), F32),
        compiler_params=_params(("arbitrary", "arbitrary", "arbitrary")),
        name="diff_attn_prompt",
    )(qi_tab, ki_tab, q, q, kb, kb, vb, lam_params)


def _decode_kernel(q_ref, ck_ref, cv_ref, kn_ref, vn_ref, lam_ref, o_ref, m_ref, l_ref, acc_ref,
                   *, lam_init, n_cache_tiles):
    t = pl.program_id(1)
    tq = q_ref.shape[0]

    @pl.when(t == 0)
    def _():
        m_ref[...] = jnp.full(m_ref.shape, NEG_INF, F32)
        l_ref[...] = jnp.zeros(l_ref.shape, F32)
        acc_ref[...] = jnp.zeros(acc_ref.shape, F32)

    def update(kfn, vfn):
        lane = lax.broadcasted_iota(I32, (tq, LANES), 1)
        for j in range(H_A // 2):
            for mp in range(2):
                col = mp * (D_MODEL // 2) + j * LANES
                q = q_ref[:, col:col + LANES]
                k = kfn(col)
                for a in range(2):
                    h = 2 * j + a
                    qa = jnp.where((lane >= DA) == bool(a), q, jnp.zeros_like(q))
                    s = _nt_dot(qa, k)
                    _softmax_update(s, vfn(h * DV_A), m_ref, l_ref, acc_ref, mp * H_A + h)

    @pl.when(t < n_cache_tiles)
    def _():
        update(lambda c: ck_ref[0, :, c:c + LANES].astype(BF16),
               lambda c: cv_ref[0, :, c:c + LANES].astype(BF16))

    @pl.when(t == n_cache_tiles)
    def _():
        update(lambda c: kn_ref[:, c:c + LANES].astype(BF16),
               lambda c: vn_ref[:, c:c + LANES].astype(BF16))
        lam = _diff_lambda(lam_ref, lam_init)
        for h in range(H_A):
            o0 = acc_ref[h] / l_ref[h]
            o1 = acc_ref[H_A + h] / l_ref[H_A + h]
            o_ref[:, h * DV_A:(h + 1) * DV_A] = o0 - lam * o1


def _decode_call(q, cache_k, cache_v, k_new, v_new, lam_params, *, lam_init, tk=512):
    nb, P, _ = cache_k.shape
    S = q.shape[0] // nb
    nt = P // tk
    return pl.pallas_call(
        functools.partial(_decode_kernel, lam_init=lam_init, n_cache_tiles=nt),
        grid=(nb, nt + 1),
        in_specs=[pl.BlockSpec((S, D_MODEL), lambda b, t: (b, 0)),
                  pl.BlockSpec((1, tk, D_MODEL), lambda b, t: (b, jnp.minimum(t, nt - 1), 0)),
                  pl.BlockSpec((1, tk, D_MODEL), lambda b, t: (b, jnp.minimum(t, nt - 1), 0)),
                  pl.BlockSpec((S, D_MODEL), lambda b, t: (b, 0)),
                  pl.BlockSpec((S, D_MODEL), lambda b, t: (b, 0)),
                  pl.BlockSpec((4, DA), lambda b, t: (0, 0))],
        out_specs=pl.BlockSpec((S, D_MODEL), lambda b, t: (b, 0)),
        out_shape=jax.ShapeDtypeStruct((nb * S, D_MODEL), F32),
        scratch_shapes=[pltpu.VMEM((2 * H_A, S, 1), F32), pltpu.VMEM((2 * H_A, S, 1), F32),
                        pltpu.VMEM((2 * H_A, S, DV_A), F32)],
        compiler_params=_params(("arbitrary", "arbitrary")),
        name="diff_attn_sample",
    )(q, cache_k, cache_v, k_new, v_new, lam_params)


def _out_kernel(o_ref, sub_ref, w_ref, h_ref, g_ref, out_ref, *, norm_scale):
    o = o_ref[...]
    if norm_scale is not None:
        parts = []
        for s in range(D_MODEL // DV_A):
            z = o[:, s * DV_A:(s + 1) * DV_A]
            ms = jnp.mean(z * z, axis=-1, keepdims=True)
            parts.append((z * lax.rsqrt(ms + NORM_EPS) * sub_ref[...] * norm_scale).astype(BF16))
        ob = jnp.concatenate(parts, axis=-1)
    else:
        ob = o.astype(BF16)
    mix = jnp.dot(ob, w_ref[...], preferred_element_type=F32)
    out_ref[...] = h_ref[...] + g_ref[0] * mix


def _out_call(o, sub, w, h, gate, *, rows_per_batch, tm, norm_scale, name):
    T, D = h.shape
    ni = rows_per_batch // tm
    return pl.pallas_call(
        functools.partial(_out_kernel, norm_scale=norm_scale),
        grid=(T // tm,),
        in_specs=[pl.BlockSpec((tm, D), lambda i: (i, 0)),
                  pl.BlockSpec((1, DV_A), lambda i: (0, 0)),
                  pl.BlockSpec((D, D), lambda i: (0, 0)),
                  pl.BlockSpec((tm, D), lambda i: (i, 0)),
                  pl.BlockSpec((1, 1, D), lambda i: (i // ni, 0, 0))],
        out_specs=pl.BlockSpec((tm, D), lambda i: (i, 0)),
        out_shape=jax.ShapeDtypeStruct((T, D), F32),
        compiler_params=_params(("arbitrary",)),
        name=name,
    )(o, sub, w, h, gate)


def _split_bf16(x):
    hi = x.astype(BF16)
    lo = (x - hi.astype(F32)).astype(BF16)
    return hi, lo


def _router_kernel(h_ref, g_ref, sh_ref, sc_ref, w_ref, b_ref, carry_ref,
                   meta_ref, gate_ref, cnt_ref, run_ref):
    i = pl.program_id(0)
    tm = h_ref.shape[0]

    @pl.when(i == 0)
    def _():
        run_ref[...] = carry_ref[...]

    xn = _modnorm(h_ref[...], g_ref[...], sh_ref[0], sc_ref[0])
    x_hi, x_lo = _split_bf16(xn)
    w_hi, w_lo = _split_bf16(w_ref[...])
    logits = (jnp.dot(x_hi, w_hi, preferred_element_type=F32)
              + jnp.dot(x_lo, w_hi, preferred_element_type=F32)
              + jnp.dot(x_hi, w_lo, preferred_element_type=F32)
              + jnp.dot(x_lo, w_lo, preferred_element_type=F32)) + b_ref[...]
    lane_i = lax.broadcasted_iota(I32, (tm, LANES), 1)
    lane = lane_i.astype(F32)
    work = jnp.where(lane_i < N_EXPERTS, logits, -jnp.inf)
    vals, idxs = [], []
    sel = jnp.zeros((tm, LANES), F32)
    for _ in range(TOP_K):
        mx = jnp.max(work, axis=-1, keepdims=True)
        ix = jnp.min(jnp.where(work == mx, lane, float(LANES)), axis=-1, keepdims=True)
        hit = lane == ix
        sel = sel + jnp.where(hit, 1.0, 0.0)
        work = jnp.where(hit, -jnp.inf, work)
        vals.append(mx)
        idxs.append(ix)
    es = [jnp.exp(v - vals[0]) for v in vals]
    den = es[0] + es[1] + es[2] + es[3]
    r = lax.broadcasted_iota(I32, (tm, tm), 0)
    c = lax.broadcasted_iota(I32, (tm, tm), 1)
    lower = jnp.where(c < r, 1.0, 0.0).astype(BF16)
    before = jnp.dot(lower, sel.astype(BF16), preferred_element_type=F32) + run_ref[...]
    meta = jnp.zeros((tm, LANES), I32)
    gate = jnp.zeros((tm, LANES), F32)
    for k in range(TOP_K):
        rank = jnp.sum(jnp.where(lane == idxs[k], before, 0.0), axis=-1, keepdims=True)
        meta = jnp.where(lane_i == k, idxs[k].astype(I32), meta)
        meta = jnp.where(lane_i == TOP_K + k, rank.astype(I32), meta)
        gate = jnp.where(lane_i == k, es[k] / den, gate)
    meta_ref[...] = meta
    gate_ref[...] = gate
    run_ref[...] = run_ref[...] + jnp.sum(sel, axis=0, keepdims=True)
    cnt_ref[...] = run_ref[...]


def _router_call(h, gain, shift, scale, w_pad, b_pad, carry, *, rows_per_batch, tm, name):
    T, D = h.shape
    ni = rows_per_batch // tm
    return pl.pallas_call(
        _router_kernel,
        grid=(T // tm,),
        in_specs=[pl.BlockSpec((tm, D), lambda i: (i, 0)),
                  pl.BlockSpec((1, D), lambda i: (0, 0)),
                  pl.BlockSpec((1, 1, D), lambda i: (i // ni, 0, 0)),
                  pl.BlockSpec((1, 1, D), lambda i: (i // ni, 0, 0)),
                  pl.BlockSpec((D, LANES), lambda i: (0, 0)),
                  pl.BlockSpec((1, LANES), lambda i: (0, 0)),
                  pl.BlockSpec((1, LANES), lambda i: (0, 0))],
        out_specs=[pl.BlockSpec((tm, LANES), lambda i: (i, 0)),
                   pl.BlockSpec((tm, LANES), lambda i: (i, 0)),
                   pl.BlockSpec((1, LANES), lambda i: (0, 0))],
        out_shape=[jax.ShapeDtypeStruct((T, LANES), I32),
                   jax.ShapeDtypeStruct((T, LANES), F32),
                   jax.ShapeDtypeStruct((1, LANES), F32)],
        scratch_shapes=[pltpu.VMEM((1, LANES), F32)],
        compiler_params=_params(("arbitrary",)),
        name=name,
    )(h, gain, shift, scale, w_pad, b_pad, carry)


def _scatter_kernel(fill_ref, h_ref, g_ref, sh_ref, sc_ref, slot_ref, xbuf_ref,
                    xn_ref, zero_ref, sem, *, group):
    i = pl.program_id(0)
    tm = h_ref.shape[0]

    @pl.when(i == 0)
    def _():
        zero_ref[...] = jnp.zeros(zero_ref.shape, F32)
        for e in range(fill_ref.shape[0]):
            @pl.when(fill_ref[e] >= 0)
            def _():
                row = pl.multiple_of(fill_ref[e], group)
                pltpu.make_async_copy(zero_ref, xbuf_ref.at[pl.ds(row, group)], sem).start()
        for e in range(fill_ref.shape[0]):
            @pl.when(fill_ref[e] >= 0)
            def _():
                pltpu.make_async_copy(zero_ref, xbuf_ref.at[pl.ds(0, group)], sem).wait()

    xn_ref[...] = _modnorm(h_ref[...], g_ref[...], sh_ref[0], sc_ref[0])

    def issue(r, carry):
        for k in range(TOP_K):
            dst = slot_ref[0, 0, r * TOP_K + k]
            pltpu.make_async_copy(xn_ref.at[pl.ds(r, 1)], xbuf_ref.at[pl.ds(dst, 1)], sem).start()
        return carry

    lax.fori_loop(0, tm, issue, 0, unroll=8)

    def drain(r, carry):
        pltpu.make_async_copy(xn_ref.at[pl.ds(0, 1)], xbuf_ref.at[pl.ds(0, 1)], sem).wait()
        return carry

    lax.fori_loop(0, tm * TOP_K, drain, 0, unroll=8)


def _scatter_call(h, gain, shift, scale, slots, fills, *, rows_per_batch, tm, n_rows, group, name):
    T, D = h.shape
    ni = rows_per_batch // tm
    grid_spec = pltpu.PrefetchScalarGridSpec(
        num_scalar_prefetch=1,
        grid=(T // tm,),
        in_specs=[pl.BlockSpec((tm, D), lambda i, t: (i, 0)),
                  pl.BlockSpec((1, D), lambda i, t: (0, 0)),
                  pl.BlockSpec((1, 1, D), lambda i, t: (i // ni, 0, 0)),
                  pl.BlockSpec((1, 1, D), lambda i, t: (i // ni, 0, 0)),
                  pl.BlockSpec((1, 1, tm * TOP_K), lambda i, t: (i, 0, 0),
                               memory_space=pltpu.SMEM)],
        out_specs=pl.BlockSpec(memory_space=pl.ANY),
        scratch_shapes=[pltpu.VMEM((tm, D), F32), pltpu.VMEM((group, D), F32),
                        pltpu.SemaphoreType.DMA(())],
    )
    return pl.pallas_call(
        functools.partial(_scatter_kernel, group=group),
        grid_spec=grid_spec,
        out_shape=jax.ShapeDtypeStruct((n_rows, D), F32),
        compiler_params=_params(("arbitrary",)),
        name=name,
    )(fills, h, gain, shift, scale, slots)


def _expert_kernel(be_ref, nu_ref, x_ref, wgu_ref, bgu_ref, wd_ref, bd_ref, y_ref):
    b = pl.program_id(0)

    @pl.when(b < nu_ref[0])
    def _():
        x = x_ref[...].astype(BF16)
        hgu = jnp.dot(x, wgu_ref[0], preferred_element_type=F32) + bgu_ref[0]
        glu = jnp.minimum(hgu[:, :D_FF], SWIGLU_LIMIT)
        lin = jnp.clip(hgu[:, D_FF:], -SWIGLU_LIMIT, SWIGLU_LIMIT)
        act = glu * jax.nn.sigmoid(SWIGLU_ALPHA * glu) * (lin + 1.0)
        y_ref[...] = jnp.dot(act.astype(BF16), wd_ref[0], preferred_element_type=F32) + bd_ref[0]

    @pl.when(b >= nu_ref[0])
    def _():
        y_ref[...] = jnp.zeros(y_ref.shape, F32)


def _expert_call(xbuf, block_expert, n_used, wgu, bgu, wd, bd, *, group, name):
    n_rows, D = xbuf.shape
    nblk = n_rows // group
    row_map = lambda b, be, nu: (jnp.minimum(b, nu[0] - 1), 0)
    grid_spec = pltpu.PrefetchScalarGridSpec(
        num_scalar_prefetch=2,
        grid=(nblk,),
        in_specs=[pl.BlockSpec((group, D), row_map),
                  pl.BlockSpec((1, D, 2 * D_FF), lambda b, be, nu: (be[b], 0, 0)),
                  pl.BlockSpec((1, 1, 2 * D_FF), lambda b, be, nu: (be[b], 0, 0)),
                  pl.BlockSpec((1, D_FF, D), lambda b, be, nu: (be[b], 0, 0)),
                  pl.BlockSpec((1, 1, D), lambda b, be, nu: (be[b], 0, 0))],
        out_specs=pl.BlockSpec((group, D), lambda b, be, nu: (b, 0)),
    )
    return pl.pallas_call(
        _expert_kernel,
        grid_spec=grid_spec,
        out_shape=jax.ShapeDtypeStruct((n_rows, D), F32),
        compiler_params=_params(("arbitrary",)),
        name=name,
    )(block_expert, n_used, xbuf, wgu, bgu, wd, bd)


def _combine_kernel(y_ref, slot_ref, gate_ref, h_ref, g_ref, out_ref, buf_ref, sem):
    tm = h_ref.shape[0]

    def issue(r, carry):
        for k in range(TOP_K):
            src = slot_ref[0, 0, r * TOP_K + k]
            pltpu.make_async_copy(y_ref.at[pl.ds(src, 1)], buf_ref.at[k, pl.ds(r, 1)], sem).start()
        return carry

    lax.fori_loop(0, tm, issue, 0, unroll=8)

    def drain(r, carry):
        pltpu.make_async_copy(y_ref.at[pl.ds(0, 1)], buf_ref.at[0, pl.ds(0, 1)], sem).wait()
        return carry

    lax.fori_loop(0, tm * TOP_K, drain, 0, unroll=8)

    gate = gate_ref[...]
    ff = gate[:, 0:1] * buf_ref[0]
    for k in range(1, TOP_K):
        ff = ff + gate[:, k:k + 1] * buf_ref[k]
    out_ref[...] = h_ref[...] + g_ref[0] * ff


def _combine_call(ybuf, slots, gates, h, gate2, *, rows_per_batch, tm, name):
    T, D = h.shape
    ni = rows_per_batch // tm
    return pl.pallas_call(
        _combine_kernel,
        grid=(T // tm,),
        in_specs=[pl.BlockSpec(memory_space=pl.ANY),
                  pl.BlockSpec((1, 1, tm * TOP_K), lambda i: (i, 0, 0), memory_space=pltpu.SMEM),
                  pl.BlockSpec((tm, LANES), lambda i: (i, 0)),
                  pl.BlockSpec((tm, D), lambda i: (i, 0)),
                  pl.BlockSpec((1, 1, D), lambda i: (i // ni, 0, 0))],
        out_specs=pl.BlockSpec((tm, D), lambda i: (i, 0)),
        out_shape=jax.ShapeDtypeStruct((T, D), F32),
        scratch_shapes=[pltpu.VMEM((TOP_K, tm, D), F32), pltpu.SemaphoreType.DMA(())],
        compiler_params=_params(("arbitrary",)),
        name=name,
    )(ybuf, slots, gates, h, gate2)


def _moe(h, gain, shift, scale, gate2, w_router_pad, b_router_pad, wgu, bgu, wd, bd, *,
         rows_per_batch, tm, group, tag):
    T, D = h.shape
    A = T * TOP_K
    carry = jnp.zeros((1, LANES), F32)
    meta, gates, counts = _router_call(h, gain, shift, scale, w_router_pad, b_router_pad, carry,
                                       rows_per_batch=rows_per_batch, tm=tm, name="router_" + tag)
    counts = counts[0, :N_EXPERTS].astype(I32)
    padded = (counts + group - 1) // group * group
    ends = jnp.cumsum(padded)
    pstart = ends - padded
    nblk = -(-(A + N_EXPERTS * (group - 1)) // group)
    n_rows = nblk * group
    block_expert = jnp.minimum(
        jnp.searchsorted(ends, jnp.arange(nblk, dtype=I32) * group, side='right'),
        N_EXPERTS - 1).astype(I32)
    n_used = jnp.maximum(ends[-1:] // group, 1).astype(I32)
    tails = jnp.where(padded > 0, ends - group, -1)
    spare = jnp.arange(A // group, nblk, dtype=I32)
    fills = jnp.concatenate([tails, jnp.where(spare >= n_used[0], spare * group, -1)]).astype(I32)
    idx = meta[:, :TOP_K]
    rank = meta[:, TOP_K:2 * TOP_K]
    onehot = idx[:, :, None] == jnp.arange(N_EXPERTS, dtype=I32)[None, None, :]
    slot = jnp.sum(jnp.where(onehot, pstart[None, None, :], 0), axis=-1) + rank
    slots = slot.reshape(T // tm, 1, tm * TOP_K).astype(I32)

    xbuf = _scatter_call(h, gain, shift, scale, slots, fills, rows_per_batch=rows_per_batch,
                         tm=tm, n_rows=n_rows, group=group, name="dispatch_" + tag)
    ybuf = _expert_call(xbuf, block_expert, n_used, wgu, bgu, wd, bd, group=group,
                        name="experts_" + tag)
    return _combine_call(ybuf, slots, gates, h, gate2, rows_per_batch=rows_per_batch, tm=tm,
                         name="combine_" + tag)


def _band_kernel(q_ref, k_ref, v_ref, bias_ref, o_ref, *, window):
    qi = pl.program_id(2)
    tq = q_ref.shape[0]
    start = pl.multiple_of(qi * tq, tq)
    kw = k_ref[0, pl.ds(start, window), :]
    vw = v_ref[0, pl.ds(start, window), :]
    q = q_ref[...]
    lane = lax.broadcasted_iota(I32, (tq, LANES), 1)
    col = lax.broadcasted_iota(I32, (tq, window), 1)
    real = col >= BAND_PAST - start
    outs = []
    for a in range(2):
        qa = jnp.where((lane >= DB) == bool(a), q, jnp.zeros_like(q))
        s = _nt_dot(qa, kw) + bias_ref[a]
        s = jnp.where(real, s, NEG_INF)
        m = jnp.max(s, axis=-1, keepdims=True)
        p = jnp.exp(s - m)
        l = jnp.sum(p, axis=-1, keepdims=True)
        outs.append(jnp.dot(p.astype(BF16), vw, preferred_element_type=F32) / l)
    o_ref[...] = jnp.where(lane < DB, outs[0], outs[1])


def _band_call(q, kpad, vpad, bias, *, nb, seq, tq=128):
    T = q.shape[0]
    nq = seq // tq
    window = tq + BAND_PAST
    npair = H_B // 2
    return pl.pallas_call(
        functools.partial(_band_kernel, window=window),
        grid=(nb, npair, nq),
        in_specs=[pl.BlockSpec((tq, LANES), lambda b, j, i: (b * nq + i, j)),
                  pl.BlockSpec((1, BAND_PAST + seq, LANES), lambda b, j, i: (b, 0, j)),
                  pl.BlockSpec((1, BAND_PAST + seq, LANES), lambda b, j, i: (b, 0, j)),
                  pl.BlockSpec((2, tq, window), lambda b, j, i: (j, 0, 0))],
        out_specs=pl.BlockSpec((tq, LANES), lambda b, j, i: (b * nq + i, j)),
        out_shape=jax.ShapeDtypeStruct((T, D_MODEL), F32),
        compiler_params=_params(("arbitrary", "arbitrary", "arbitrary")),
        name="band_attn_prompt",
    )(q, kpad, vpad, bias)


def _band_dec_kernel(q_ref, ck_ref, cv_ref, kn_ref, vn_ref, bc_ref, bn_ref, o_ref):
    tq = q_ref.shape[0]
    lane = lax.broadcasted_iota(I32, (tq, LANES), 1)
    for j in range(H_B // 2):
        cs = slice(j * LANES, (j + 1) * LANES)
        q = q_ref[:, cs]
        kc = ck_ref[0, :, cs].astype(BF16)
        vc = cv_ref[0, :, cs].astype(BF16)
        kn = kn_ref[:, cs].astype(BF16)
        vn = vn_ref[:, cs].astype(BF16)
        outs = []
        for a in range(2):
            h = 2 * j + a
            qa = jnp.where((lane >= DB) == bool(a), q, jnp.zeros_like(q))
            sc = _nt_dot(qa, kc) + bc_ref[h]
            sn = _nt_dot(qa, kn) + bn_ref[h]
            m = jnp.maximum(jnp.max(sc, axis=-1, keepdims=True), jnp.max(sn, axis=-1, keepdims=True))
            pc = jnp.exp(sc - m)
            pn = jnp.exp(sn - m)
            l = jnp.sum(pc, axis=-1, keepdims=True) + jnp.sum(pn, axis=-1, keepdims=True)
            o = (jnp.dot(pc.astype(BF16), vc, preferred_element_type=F32)
                 + jnp.dot(pn.astype(BF16), vn, preferred_element_type=F32))
            outs.append(o / l)
        o_ref[:, cs] = jnp.where(lane < DB, outs[0], outs[1])


def _band_dec_call(q, cache_k, cache_v, k_new, v_new, bias_c, bias_n):
    nb, R, _ = cache_k.shape
    S = q.shape[0] // nb
    return pl.pallas_call(
        _band_dec_kernel,
        grid=(nb,),
        in_specs=[pl.BlockSpec((S, D_MODEL), lambda b: (b, 0)),
                  pl.BlockSpec((1, R, D_MODEL), lambda b: (b, 0, 0)),
                  pl.BlockSpec((1, R, D_MODEL), lambda b: (b, 0, 0)),
                  pl.BlockSpec((S, D_MODEL), lambda b: (b, 0)),
                  pl.BlockSpec((S, D_MODEL), lambda b: (b, 0)),
                  pl.BlockSpec(bias_c.shape, lambda b: (0, 0, 0)),
                  pl.BlockSpec(bias_n.shape, lambda b: (0, 0, 0))],
        out_specs=pl.BlockSpec((S, D_MODEL), lambda b: (b, 0)),
        out_shape=jax.ShapeDtypeStruct((nb * S, D_MODEL), F32),
        compiler_params=_params(("arbitrary",)),
        name="band_attn_sample",
    )(q, cache_k, cache_v, k_new, v_new, bias_c, bias_n)


def _rope_tables(past_len, seq):
    half = DA // 2
    inv = ROPE_THETA ** (-jnp.arange(half, dtype=F32) / half)
    pos = (past_len + jnp.arange(seq)).astype(F32)
    ang = pos[:, None] * inv[None, :]
    cos = jnp.cos(ang)
    sin = jnp.sin(ang)
    cos_t = jnp.tile(cos, (1, LANES // half))
    sin_t = jnp.tile(jnp.concatenate([-sin, sin], axis=-1), (1, LANES // DA))
    return cos_t, sin_t


def _band_bias(table, n_q, n_past, n_k, band_mask):
    dist = np.arange(n_q)[:, None] + n_past - np.arange(n_k)[None, :]
    idx = np.clip(dist, -(CHUNK - 1), REL_MAX) + (CHUNK - 1)
    bias = table.astype(F32)[:, idx]
    if band_mask:
        cq = np.arange(n_q)[:, None] // CHUNK
        ck = np.arange(n_k)[None, :] // CHUNK
        inside = (ck >= cq) & (ck <= cq + BAND_PAST // CHUNK)
        bias = jnp.where(jnp.asarray(inside)[None], bias, NEG_INF)
    return bias


def _tile_gain(g):
    return jnp.tile(g.astype(F32), LANES // g.shape[0])[None, :]


def _deinterleave(w):
    return jnp.concatenate([w[..., 0::2], w[..., 1::2]], axis=-1)


def kernel(x_prompt, x_sample, c_prompt, c_sample, cache_a_k, cache_a_v, cache_b_k, cache_b_v, w_mod, b_mod, norm_attn, norm_ffn, w_qkv_a, q_norm_a, k_norm_a, lambda_a, subln_a, w_o_a, w_mod_kv, b_mod_kv, norm_kv, w_kv_b, k_norm_b, w_q_b, q_norm_b, rel_bias_b, w_o_b, w_router, b_router, w_gu, b_gu, w_down, b_down):
    D = D_MODEL
    nbp, seq_p, _ = x_prompt.shape
    nbs, seq_s, _ = x_sample.shape
    past_len = cache_a_k.shape[2]
    band_rows = cache_b_k.shape[1]
    depth = w_mod.shape[0]
    groups = (
        dict(tag="p", nb=nbp, seq=seq_p, tm=256, group=256, past=0),
        dict(tag="s", nb=nbs, seq=seq_s, tm=seq_s, group=128, past=past_len),
    )

    nrow = nbp + nbs
    rpad = -(-nrow // 8) * 8
    c_all = jnp.concatenate([c_prompt, c_sample, jnp.zeros((rpad - nrow, D), F32)], axis=0)
    mod = _mod_call(c_all, w_mod, b_mod[:, None, :])
    mod_kv = _mod_call(c_all, w_mod_kv[None], b_mod_kv[None, None, :])[0]

    def mod_vec(arr, g, k):
        lo = 0 if g["tag"] == "p" else nbp
        return arr[lo:lo + g["nb"], k * D:(k + 1) * D][:, None, :]

    w_qkv = w_qkv_a[0].astype(BF16)
    w_oa = w_o_a[0].astype(BF16)
    w_kv = w_kv_b.astype(BF16)
    w_qb = w_q_b[0].astype(BF16)
    w_ob = w_o_b[0].astype(BF16)
    gains_a = jnp.concatenate([_tile_gain(q_norm_a[0]), _tile_gain(k_norm_a[0]),
                               jnp.ones((1, LANES), F32)], axis=0)
    gains_kv = jnp.concatenate([_tile_gain(k_norm_b), jnp.ones((1, LANES), F32)], axis=0)
    gains_qb = _tile_gain(q_norm_b[0])
    sub_a = subln_a[0].astype(F32)[None, :]
    lam_init = 0.8 - 0.6 * math.exp(-0.3 * 0)
    wr_pad = jnp.pad(w_router, ((0, 0), (0, 0), (0, LANES - N_EXPERTS)))
    br_pad = jnp.pad(b_router, ((0, 0), (0, LANES - N_EXPERTS)))[:, None, :]
    wgu_b = _deinterleave(w_gu).astype(BF16)
    bgu_d = _deinterleave(b_gu)[:, :, None, :]
    wd_b = w_down.astype(BF16)
    bd_d = b_down[:, :, None, :]

    xs = {"p": x_prompt.reshape(nbp * seq_p, D), "s": x_sample.reshape(nbs * seq_s, D)}
    outs = {}
    for g in groups:
        tag, nb, seq, tm = g["tag"], g["nb"], g["seq"], g["tm"]
        h = xs[tag]
        sel = functools.partial(mod_vec, g=g)
        cos_t, sin_t = _rope_tables(g["past"], seq)

        l = 0
        sh1, sc1, g1, sh2, sc2, g2 = [sel(mod[l], k=k) for k in range(6)]
        q, k32, kb, v32, vb = _proj_call(
            h, norm_attn[l][None, :], sh1, sc1, w_qkv, gains_a, cos_t, sin_t,
            rows_per_batch=seq, tm=tm,
            sections=((True, True, DA ** -0.5, False, True), (True, True, 1.0, True, True),
                      (False, False, 1.0, True, True)),
            name="qkv_proj_" + tag)
        if tag == "p":
            o = _flash_call(q, kb, vb, lambda_a[0], nb=nb, seq=seq, lam_init=lam_init)
        else:
            ck = cache_a_k[0].reshape(nb, past_len, D)
            cv = cache_a_v[0].reshape(nb, past_len, D)
            o = _decode_call(q, ck, cv, k32, v32, lambda_a[0], lam_init=lam_init)
        h = _out_call(o, sub_a, w_oa, h, g1, rows_per_batch=seq, tm=tm,
                      norm_scale=1.0 - lam_init, name="attn_out_a_" + tag)
        h = _moe(h, norm_ffn[l][None, :], sh2, sc2, g2, wr_pad[l], br_pad[l],
                 wgu_b[l], bgu_d[l], wd_b[l], bd_d[l],
                 rows_per_batch=seq, tm=tm, group=g["group"], tag="l0" + tag)
        a_k = k32.reshape(1, nb, seq, 2 * H_A, DA)
        a_v = v32.reshape(1, nb, seq, H_A, DV_A)

        shk, sck = [sel(mod_kv, k=k) for k in range(2)]
        l = 1
        sh1, sc1, g1, sh2, sc2, g2 = [sel(mod[l], k=k) for k in range(6)]
        kv_sections = ((True, False, 1.0, True, tag == "p"), (False, False, 1.0, True, tag == "p"))
        if tag == "p":
            kb32, kbp, vb32, vbp = _proj_call(
                h, norm_kv[None, :], shk, sck, w_kv, gains_kv, cos_t, sin_t,
                rows_per_batch=seq, tm=BAND_PAST, sections=kv_sections, pad=True,
                name="kv_proj_" + tag)
            new_b_k = kb32[:, -band_rows:].reshape(nb, band_rows, H_B, DB)
            new_b_v = vb32[:, -band_rows:].reshape(nb, band_rows, H_B, DB)
        else:
            kb32, vb32 = _proj_call(
                h, norm_kv[None, :], shk, sck, w_kv, gains_kv, cos_t, sin_t,
                rows_per_batch=seq, tm=tm, sections=kv_sections, name="kv_proj_" + tag)
            ckb = cache_b_k.reshape(nb, band_rows, D)
            cvb = cache_b_v.reshape(nb, band_rows, D)
            new_b_k = jnp.concatenate([ckb, kb32.reshape(nb, seq, D)], axis=1)[:, -band_rows:]
            new_b_v = jnp.concatenate([cvb, vb32.reshape(nb, seq, D)], axis=1)[:, -band_rows:]
            new_b_k = new_b_k.reshape(nb, band_rows, H_B, DB)
            new_b_v = new_b_v.reshape(nb, band_rows, H_B, DB)

        (qb,) = _proj_call(
            h, norm_attn[l][None, :], sh1, sc1, w_qb, gains_qb, cos_t, sin_t,
            rows_per_batch=seq, tm=tm, sections=((True, False, DB ** -0.5, False, True),),
            name="q_proj_b_" + tag)
        if tag == "p":
            tq = 128
            bias = _band_bias(rel_bias_b[0], tq, BAND_PAST, tq + BAND_PAST, True)
            o = _band_call(qb, kbp, vbp, bias, nb=nb, seq=seq, tq=tq)
        else:
            bias = _band_bias(rel_bias_b[0], seq, band_rows, band_rows + seq, False)
            o = _band_dec_call(qb, ckb, cvb, kb32, vb32, bias[:, :, :band_rows], bias[:, :, band_rows:])
        h = _out_call(o, sub_a, w_ob, h, g1, rows_per_batch=seq, tm=tm, norm_scale=None,
                      name="attn_out_b_" + tag)
        h = _moe(h, norm_ffn[l][None, :], sh2, sc2, g2, wr_pad[l], br_pad[l],
                 wgu_b[l], bgu_d[l], wd_b[l], bd_d[l],
                 rows_per_batch=seq, tm=tm, group=g["group"], tag="l1" + tag)
        outs[tag] = (h.reshape(nb, seq, D), a_k, a_v, new_b_k, new_b_v)

    yp, akp, avp, bkp, bvp = outs["p"]
    ys, aks, avs, bks, bvs = outs["s"]
    return (yp, ys, akp, avp, bkp, bvp, aks, avs, bks, bvs)
```

```python
import functools
import math

import jax
import jax.numpy as jnp
import numpy as np
from jax import lax
from jax.experimental import pallas as pl
from jax.experimental.pallas import tpu as pltpu

F32 = jnp.float32
BF16 = jnp.bfloat16
I32 = jnp.int32

D_MODEL = 1024
CHUNK = 64
H_A = 8
DA = 64
DV_A = 2 * DA
ROPE_THETA = 10000.0
H_B = 16
DB = D_MODEL // H_B
BAND_PAST = 8 * CHUNK
REL_MAX = 128
N_EXPERTS = 32
TOP_K = 4
D_FF = D_MODEL
SWIGLU_LIMIT = 7.0
SWIGLU_ALPHA = 1.702
NORM_EPS = 1e-6
NEG_INF = -1e30
LOG2_E = math.log2(math.e)

LANES = 128
VMEM_LIMIT = 56 * 2**20


def _params(sem):
    return pltpu.CompilerParams(dimension_semantics=sem, vmem_limit_bytes=VMEM_LIMIT)


def _modnorm(x, gain, shift, scale):
    ms = jnp.mean(x * x, axis=-1, keepdims=True)
    y = x * lax.rsqrt(ms + NORM_EPS) * gain
    return y * (1.0 + scale) + shift


def _group_ones(width):
    r = lax.broadcasted_iota(I32, (LANES, LANES), 0) // width
    c = lax.broadcasted_iota(I32, (LANES, LANES), 1) // width
    return jnp.where(r == c, 1.0, 0.0).astype(BF16)


def _headnorm64(z, grp, gain):
    zz = z * z
    hi = zz.astype(BF16)
    lo = (zz - hi.astype(F32)).astype(BF16)
    ss = (jnp.dot(hi, grp, preferred_element_type=F32)
          + jnp.dot(lo, grp, preferred_element_type=F32))
    return z * lax.rsqrt(ss * (1.0 / DA) + NORM_EPS) * gain


def _rope(z, cos, sin_signed, first_half):
    fwd = pltpu.roll(z, LANES - DA // 2, axis=1)
    bwd = pltpu.roll(z, DA // 2, axis=1)
    return z * cos + jnp.where(first_half, fwd, bwd) * sin_signed


def _mod_kernel(c_ref, w_ref, b_ref, o_ref):
    c = c_ref[...]
    cs = (c * jax.nn.sigmoid(c)).astype(BF16)
    o_ref[0] = jnp.dot(cs, w_ref[0].astype(BF16), preferred_element_type=F32) + b_ref[0]


def _mod_call(c_all, w, b):
    L, D, N = w.shape
    R = c_all.shape[0]
    tn = 1024
    return pl.pallas_call(
        _mod_kernel,
        grid=(L, N // tn),
        in_specs=[pl.BlockSpec((R, D), lambda l, j: (0, 0)),
                  pl.BlockSpec((1, D, tn), lambda l, j: (l, 0, j)),
                  pl.BlockSpec((1, 1, tn), lambda l, j: (l, 0, j))],
        out_specs=pl.BlockSpec((1, R, tn), lambda l, j: (l, 0, j)),
        out_shape=jax.ShapeDtypeStruct((L, R, N), F32),
        compiler_params=_params(("arbitrary", "arbitrary")),
        name="adaln_mod",
    )(c_all, w, b)


def _proj_kernel(*refs, sections, pad):
    x_ref, g_ref, sh_ref, sc_ref, w_ref, gains_ref, cos_ref, sin_ref = refs[:8]
    out_refs = refs[8:]
    tm = x_ref.shape[0]

    def compute():
        xn = _modnorm(x_ref[...], g_ref[...], sh_ref[0], sc_ref[0])
        y = jnp.dot(xn.astype(BF16), w_ref[...], preferred_element_type=F32)
        grp = _group_ones(DA)
        lane = lax.broadcasted_iota(I32, (tm, LANES), 1)
        first_half = (lane % DA) < (DA // 2)
        oi = 0
        for si, (norm, rope, scale, want_f32, want_bf16) in enumerate(sections):
            outs = []
            if want_f32:
                outs.append(out_refs[oi]); oi += 1
            if want_bf16:
                outs.append(out_refs[oi]); oi += 1
            for s in range(D_MODEL // LANES):
                col = si * D_MODEL + s * LANES
                z = y[:, col:col + LANES]
                if norm:
                    z = _headnorm64(z, grp, gains_ref[si:si + 1, :])
                if rope:
                    z = _rope(z, cos_ref[...], sin_ref[...], first_half)
                if scale != 1.0:
                    z = z * scale
                for o in outs:
                    if pad:
                        o[0, :, s * LANES:(s + 1) * LANES] = z.astype(o.dtype)
                    else:
                        o[:, s * LANES:(s + 1) * LANES] = z.astype(o.dtype)

    if pad:
        i = pl.program_id(1)

        @pl.when(i == 0)
        def _():
            for o in out_refs:
                o[...] = jnp.zeros(o.shape, o.dtype)

        pl.when(i > 0)(compute)
    else:
        compute()


def _proj_call(x, gain, shift, scale, w, gains, cos, sin, *, rows_per_batch, tm, sections,
               pad=False, name):
    T, D = x.shape
    N = w.shape[1]
    nb = T // rows_per_batch
    ni = rows_per_batch // tm
    out_shapes, out_specs = [], []
    for (norm, rope, scl, want_f32, want_bf16) in sections:
        for want, dt in ((want_f32, F32), (want_bf16, BF16)):
            if not want:
                continue
            if pad:
                out_shapes.append(jax.ShapeDtypeStruct((nb, BAND_PAST + rows_per_batch, D_MODEL), dt))
                out_specs.append(pl.BlockSpec((1, tm, D_MODEL), lambda b, i: (b, i, 0)))
            else:
                out_shapes.append(jax.ShapeDtypeStruct((T, D_MODEL), dt))
                out_specs.append(pl.BlockSpec((tm, D_MODEL), lambda b, i: (b * ni + i, 0)))
    if pad:
        assert tm == BAND_PAST
        grid = (nb, ni + 1)
        xmap = lambda b, i: (b * ni + jnp.maximum(i - 1, 0), 0)
        cmap = lambda b, i: (jnp.maximum(i - 1, 0), 0)
    else:
        grid = (nb, ni)
        xmap = lambda b, i: (b * ni + i, 0)
        cmap = lambda b, i: (i, 0)
    kern = functools.partial(_proj_kernel, sections=sections, pad=pad)
    return pl.pallas_call(
        kern,
        grid=grid,
        in_specs=[pl.BlockSpec((tm, D), xmap),
                  pl.BlockSpec((1, D), lambda b, i: (0, 0)),
                  pl.BlockSpec((1, 1, D), lambda b, i: (b, 0, 0)),
                  pl.BlockSpec((1, 1, D), lambda b, i: (b, 0, 0)),
                  pl.BlockSpec((D, N), lambda b, i: (0, 0)),
                  pl.BlockSpec(gains.shape, lambda b, i: (0, 0)),
                  pl.BlockSpec((tm, LANES), cmap),
                  pl.BlockSpec((tm, LANES), cmap)],
        out_specs=out_specs,
        out_shape=out_shapes,
        compiler_params=_params(("arbitrary", "arbitrary")),
        name=name,
    )(x, gain, shift, scale, w, gains, cos, sin)


def _diff_lambda(lam_ref, lam_init):
    lp = lam_ref[...]
    s1 = jnp.sum(lp[0:1, :] * lp[1:2, :], axis=-1, keepdims=True)
    s2 = jnp.sum(lp[2:3, :] * lp[3:4, :], axis=-1, keepdims=True)
    return jnp.exp(s1) - jnp.exp(s2) + lam_init


def _with_ones(v):
    return jnp.concatenate([v, jnp.ones((v.shape[0], LANES), v.dtype)], axis=-1)


def _softmax_update(s, v_ext, m_ref, l_ref, acc_ref, c):
    tk = s.shape[1]
    width = min(tk, LANES)
    slabs = [s[:, j * width:(j + 1) * width] for j in range(tk // width)]
    mx = slabs[0]
    for sl in slabs[1:]:
        mx = jnp.maximum(mx, sl)
    m_old = m_ref[c]
    m_new = jnp.maximum(m_old, jnp.max(mx, axis=-1, keepdims=True))
    alpha = jnp.exp2(m_old - m_new)
    m_cut = m_new[:, :width]
    p = jnp.concatenate([jnp.exp2(sl - m_cut).astype(BF16) for sl in slabs], axis=-1)
    pv = jnp.dot(p, v_ext, preferred_element_type=F32)
    acc_ref[c] = alpha * acc_ref[c] + pv[:, :LANES]
    l_ref[c] = alpha * l_ref[c] + pv[:, LANES:]
    m_ref[c] = m_new


def _nt_dot(a, b):
    return lax.dot_general(a, b, (((1,), (1,)), ((), ())), preferred_element_type=F32)


def _flash_kernel(qi_tab, ki_tab, q0_ref, q1_ref, k0_ref, k1_ref, v_ref, lam_ref, o_ref,
                  m_ref, l_ref, acc_ref, *, lam_init):
    step = pl.program_id(2)
    qi = qi_tab[step]
    ki = ki_tab[step]
    tq = q0_ref.shape[0]
    tk = k0_ref.shape[0]
    ratio = tq // tk
    first_diag = ratio * qi
    last = first_diag + ratio - 1

    @pl.when(ki == 0)
    def _():
        m_ref[...] = jnp.full(m_ref.shape, NEG_INF, F32)
        l_ref[...] = jnp.zeros(l_ref.shape, F32)
        acc_ref[...] = jnp.zeros(acc_ref.shape, F32)

    def update(masked):
        lane = lax.broadcasted_iota(I32, (tq, LANES), 1)
        if masked:
            rq = lax.broadcasted_iota(I32, (tq, tk), 0) // CHUNK
            ck = lax.broadcasted_iota(I32, (tq, tk), 1) // CHUNK + (ki - first_diag) * (tk // CHUNK)
            visible = ck <= rq
        v_exts = [_with_ones(v_ref[:, a * DV_A:(a + 1) * DV_A]) for a in range(2)]
        for mp, (q_ref, k_ref) in enumerate(((q0_ref, k0_ref), (q1_ref, k1_ref))):
            q = q_ref[...]
            k = k_ref[...]
            for a in range(2):
                qa = jnp.where((lane >= DA) == bool(a), q, jnp.zeros_like(q))
                s = _nt_dot(qa, k)
                if masked:
                    s = jnp.where(visible, s, NEG_INF)
                _softmax_update(s, v_exts[a], m_ref, l_ref, acc_ref, mp * 2 + a)

    pl.when(ki < first_diag)(lambda: update(False))
    pl.when(ki >= first_diag)(lambda: update(True))

    @pl.when(ki == last)
    def _():
        lam = _diff_lambda(lam_ref, lam_init)
        for a in range(2):
            o0 = acc_ref[a] / l_ref[a]
            o1 = acc_ref[2 + a] / l_ref[2 + a]
            o_ref[:, a * DV_A:(a + 1) * DV_A] = o0 - lam * o1


def _flash_call(q, kb, vb, lam_params, *, nb, seq, lam_init, tq=1024, tk=512):
    T = q.shape[0]
    nq = seq // tq
    nk = seq // tk
    ratio = tq // tk
    pairs = [(qi, ki) for qi in range(nq) for ki in range(ratio * (qi + 1))]
    qi_tab = jnp.asarray(np.array([p[0] for p in pairs], np.int32))
    ki_tab = jnp.asarray(np.array([p[1] for p in pairs], np.int32))
    npair = H_A // 2
    grid_spec = pltpu.PrefetchScalarGridSpec(
        num_scalar_prefetch=2,
        grid=(nb, npair, len(pairs)),
        in_specs=[
            pl.BlockSpec((tq, LANES), lambda b, j, s, qt, kt: (b * nq + qt[s], j)),
            pl.BlockSpec((tq, LANES), lambda b, j, s, qt, kt: (b * nq + qt[s], npair + j)),
            pl.BlockSpec((tk, LANES), lambda b, j, s, qt, kt: (b * nk + kt[s], j)),
            pl.BlockSpec((tk, LANES), lambda b, j, s, qt, kt: (b * nk + kt[s], npair + j)),
            pl.BlockSpec((tk, 2 * DV_A), lambda b, j, s, qt, kt: (b * nk + kt[s], j)),
            pl.BlockSpec((4, DA), lambda b, j, s, qt, kt: (0, 0)),
        ],
        out_specs=pl.BlockSpec((tq, 2 * DV_A), lambda b, j, s, qt, kt: (b * nq + qt[s], j)),
        scratch_shapes=[pltpu.VMEM((4, tq, LANES), F32), pltpu.VMEM((4, tq, LANES), F32),
                        pltpu.VMEM((4, tq, DV_A), F32)],
    )
    return pl.pallas_call(
        functools.partial(_flash_kernel, lam_init=lam_init),
        grid_spec=grid_spec,
        out_shape=jax.ShapeDtypeStruct((T, D_MODEL), F32),
        compiler_params=_params(("arbitrary", "arbitrary", "arbitrary")),
        name="diff_attn_prompt",
    )(qi_tab, ki_tab, q, q, kb, kb, vb, lam_params)


def _decode_kernel(q_ref, ck_ref, cv_ref, kn_ref, vn_ref, lam_ref, o_ref, m_ref, l_ref, acc_ref,
                   *, lam_init, n_cache_tiles):
    t = pl.program_id(1)
    tq = q_ref.shape[0]

    @pl.when(t == 0)
    def _():
        m_ref[...] = jnp.full(m_ref.shape, NEG_INF, F32)
        l_ref[...] = jnp.zeros(l_ref.shape, F32)
        acc_ref[...] = jnp.zeros(acc_ref.shape, F32)

    def update(kfn, vfn):
        lane = lax.broadcasted_iota(I32, (tq, LANES), 1)
        for j in range(H_A // 2):
            for mp in range(2):
                col = mp * (D_MODEL // 2) + j * LANES
                q = q_ref[:, col:col + LANES]
                k = kfn(col)
                for a in range(2):
                    h = 2 * j + a
                    qa = jnp.where((lane >= DA) == bool(a), q, jnp.zeros_like(q))
                    s = _nt_dot(qa, k)
                    _softmax_update(s, _with_ones(vfn(h * DV_A)), m_ref, l_ref, acc_ref,
                                    mp * H_A + h)

    @pl.when(t < n_cache_tiles)
    def _():
        update(lambda c: ck_ref[0, :, c:c + LANES].astype(BF16),
               lambda c: cv_ref[0, :, c:c + LANES].astype(BF16))

    @pl.when(t == n_cache_tiles)
    def _():
        update(lambda c: kn_ref[:, c:c + LANES].astype(BF16),
               lambda c: vn_ref[:, c:c + LANES].astype(BF16))
        lam = _diff_lambda(lam_ref, lam_init)
        for h in range(H_A):
            o0 = acc_ref[h] / l_ref[h]
            o1 = acc_ref[H_A + h] / l_ref[H_A + h]
            o_ref[:, h * DV_A:(h + 1) * DV_A] = o0 - lam * o1


def _decode_call(q, cache_k, cache_v, k_new, v_new, lam_params, *, lam_init, tk=512):
    nb, P, _ = cache_k.shape
    S = q.shape[0] // nb
    nt = P // tk
    return pl.pallas_call(
        functools.partial(_decode_kernel, lam_init=lam_init, n_cache_tiles=nt),
        grid=(nb, nt + 1),
        in_specs=[pl.BlockSpec((S, D_MODEL), lambda b, t: (b, 0)),
                  pl.BlockSpec((1, tk, D_MODEL), lambda b, t: (b, jnp.minimum(t, nt - 1), 0)),
                  pl.BlockSpec((1, tk, D_MODEL), lambda b, t: (b, jnp.minimum(t, nt - 1), 0)),
                  pl.BlockSpec((S, D_MODEL), lambda b, t: (b, 0)),
                  pl.BlockSpec((S, D_MODEL), lambda b, t: (b, 0)),
                  pl.BlockSpec((4, DA), lambda b, t: (0, 0))],
        out_specs=pl.BlockSpec((S, D_MODEL), lambda b, t: (b, 0)),
        out_shape=jax.ShapeDtypeStruct((nb * S, D_MODEL), F32),
        scratch_shapes=[pltpu.VMEM((2 * H_A, S, LANES), F32), pltpu.VMEM((2 * H_A, S, LANES), F32),
                        pltpu.VMEM((2 * H_A, S, DV_A), F32)],
        compiler_params=_params(("arbitrary", "arbitrary")),
        name="diff_attn_sample",
    )(q, cache_k, cache_v, k_new, v_new, lam_params)


def _out_kernel(o_ref, sub_ref, w_ref, h_ref, g_ref, out_ref, *, norm_scale):
    o = o_ref[...]
    if norm_scale is not None:
        parts = []
        for s in range(D_MODEL // DV_A):
            z = o[:, s * DV_A:(s + 1) * DV_A]
            ms = jnp.mean(z * z, axis=-1, keepdims=True)
            parts.append((z * lax.rsqrt(ms + NORM_EPS) * sub_ref[...] * norm_scale).astype(BF16))
        ob = jnp.concatenate(parts, axis=-1)
    else:
        ob = o.astype(BF16)
    mix = jnp.dot(ob, w_ref[...], preferred_element_type=F32)
    out_ref[...] = h_ref[...] + g_ref[0] * mix


def _out_call(o, sub, w, h, gate, *, rows_per_batch, tm, norm_scale, name):
    T, D = h.shape
    ni = rows_per_batch // tm
    return pl.pallas_call(
        functools.partial(_out_kernel, norm_scale=norm_scale),
        grid=(T // tm,),
        in_specs=[pl.BlockSpec((tm, D), lambda i: (i, 0)),
                  pl.BlockSpec((1, DV_A), lambda i: (0, 0)),
                  pl.BlockSpec((D, D), lambda i: (0, 0)),
                  pl.BlockSpec((tm, D), lambda i: (i, 0)),
                  pl.BlockSpec((1, 1, D), lambda i: (i // ni, 0, 0))],
        out_specs=pl.BlockSpec((tm, D), lambda i: (i, 0)),
        out_shape=jax.ShapeDtypeStruct((T, D), F32),
        compiler_params=_params(("arbitrary",)),
        name=name,
    )(o, sub, w, h, gate)


def _split_bf16(x):
    hi = x.astype(BF16)
    lo = (x - hi.astype(F32)).astype(BF16)
    return hi, lo


def _router_kernel(h_ref, g_ref, sh_ref, sc_ref, w_ref, b_ref, carry_ref,
                   meta_ref, gate_ref, cnt_ref, run_ref):
    i = pl.program_id(0)
    tm = h_ref.shape[0]

    @pl.when(i == 0)
    def _():
        run_ref[...] = carry_ref[...]

    xn = _modnorm(h_ref[...], g_ref[...], sh_ref[0], sc_ref[0])
    x_hi, x_lo = _split_bf16(xn)
    w_hi, w_lo = _split_bf16(w_ref[...])
    logits = (jnp.dot(x_hi, w_hi, preferred_element_type=F32)
              + jnp.dot(x_lo, w_hi, preferred_element_type=F32)
              + jnp.dot(x_hi, w_lo, preferred_element_type=F32)
              + jnp.dot(x_lo, w_lo, preferred_element_type=F32)) + b_ref[...]
    lane_i = lax.broadcasted_iota(I32, (tm, LANES), 1)
    lane = lane_i.astype(F32)
    work = jnp.where(lane_i < N_EXPERTS, logits, -jnp.inf)
    vals, idxs = [], []
    sel = jnp.zeros((tm, LANES), F32)
    for _ in range(TOP_K):
        mx = jnp.max(work, axis=-1, keepdims=True)
        ix = jnp.min(jnp.where(work == mx, lane, float(LANES)), axis=-1, keepdims=True)
        hit = lane == ix
        sel = sel + jnp.where(hit, 1.0, 0.0)
        work = jnp.where(hit, -jnp.inf, work)
        vals.append(mx)
        idxs.append(ix)
    es = [jnp.exp(v - vals[0]) for v in vals]
    den = es[0] + es[1] + es[2] + es[3]
    r = lax.broadcasted_iota(I32, (tm, tm), 0)
    c = lax.broadcasted_iota(I32, (tm, tm), 1)
    lower = jnp.where(c < r, 1.0, 0.0).astype(BF16)
    before = jnp.dot(lower, sel.astype(BF16), preferred_element_type=F32) + run_ref[...]
    meta = jnp.zeros((tm, LANES), I32)
    gate = jnp.zeros((tm, LANES), F32)
    for k in range(TOP_K):
        rank = jnp.sum(jnp.where(lane == idxs[k], before, 0.0), axis=-1, keepdims=True)
        meta = jnp.where(lane_i == k, idxs[k].astype(I32), meta)
        meta = jnp.where(lane_i == TOP_K + k, rank.astype(I32), meta)
        gate = jnp.where(lane_i == k, es[k] / den, gate)
    meta_ref[...] = meta
    gate_ref[...] = gate
    run_ref[...] = run_ref[...] + jnp.sum(sel, axis=0, keepdims=True)
    cnt_ref[...] = run_ref[...]


def _router_call(h, gain, shift, scale, w_pad, b_pad, carry, *, rows_per_batch, tm, name):
    T, D = h.shape
    ni = rows_per_batch // tm
    return pl.pallas_call(
        _router_kernel,
        grid=(T // tm,),
        in_specs=[pl.BlockSpec((tm, D), lambda i: (i, 0)),
                  pl.BlockSpec((1, D), lambda i: (0, 0)),
                  pl.BlockSpec((1, 1, D), lambda i: (i // ni, 0, 0)),
                  pl.BlockSpec((1, 1, D), lambda i: (i // ni, 0, 0)),
                  pl.BlockSpec((D, LANES), lambda i: (0, 0)),
                  pl.BlockSpec((1, LANES), lambda i: (0, 0)),
                  pl.BlockSpec((1, LANES), lambda i: (0, 0))],
        out_specs=[pl.BlockSpec((tm, LANES), lambda i: (i, 0)),
                   pl.BlockSpec((tm, LANES), lambda i: (i, 0)),
                   pl.BlockSpec((1, LANES), lambda i: (0, 0))],
        out_shape=[jax.ShapeDtypeStruct((T, LANES), I32),
                   jax.ShapeDtypeStruct((T, LANES), F32),
                   jax.ShapeDtypeStruct((1, LANES), F32)],
        scratch_shapes=[pltpu.VMEM((1, LANES), F32)],
        compiler_params=_params(("arbitrary",)),
        name=name,
    )(h, gain, shift, scale, w_pad, b_pad, carry)


def _scatter_kernel(fill_ref, h_ref, g_ref, sh_ref, sc_ref, slot_ref, xbuf_ref,
                    xn_ref, zero_ref, sem, *, group):
    i = pl.program_id(0)
    tm = h_ref.shape[0]

    @pl.when(i == 0)
    def _():
        zero_ref[...] = jnp.zeros(zero_ref.shape, F32)
        for e in range(fill_ref.shape[0]):
            @pl.when(fill_ref[e] >= 0)
            def _():
                row = pl.multiple_of(fill_ref[e], group)
                pltpu.make_async_copy(zero_ref, xbuf_ref.at[pl.ds(row, group)], sem).start()
        for e in range(fill_ref.shape[0]):
            @pl.when(fill_ref[e] >= 0)
            def _():
                pltpu.make_async_copy(zero_ref, xbuf_ref.at[pl.ds(0, group)], sem).wait()

    xn_ref[...] = _modnorm(h_ref[...], g_ref[...], sh_ref[0], sc_ref[0])

    def issue(r, carry):
        for k in range(TOP_K):
            dst = slot_ref[0, 0, r * TOP_K + k]
            pltpu.make_async_copy(xn_ref.at[pl.ds(r, 1)], xbuf_ref.at[pl.ds(dst, 1)], sem).start()
        return carry

    lax.fori_loop(0, tm, issue, 0, unroll=8)

    def drain(r, carry):
        pltpu.make_async_copy(xn_ref.at[pl.ds(0, 1)], xbuf_ref.at[pl.ds(0, 1)], sem).wait()
        return carry

    lax.fori_loop(0, tm * TOP_K, drain, 0, unroll=8)


def _scatter_call(h, gain, shift, scale, slots, fills, *, rows_per_batch, tm, n_rows, group, name):
    T, D = h.shape
    ni = rows_per_batch // tm
    grid_spec = pltpu.PrefetchScalarGridSpec(
        num_scalar_prefetch=1,
        grid=(T // tm,),
        in_specs=[pl.BlockSpec((tm, D), lambda i, t: (i, 0)),
                  pl.BlockSpec((1, D), lambda i, t: (0, 0)),
                  pl.BlockSpec((1, 1, D), lambda i, t: (i // ni, 0, 0)),
                  pl.BlockSpec((1, 1, D), lambda i, t: (i // ni, 0, 0)),
                  pl.BlockSpec((1, 1, tm * TOP_K), lambda i, t: (i, 0, 0),
                               memory_space=pltpu.SMEM)],
        out_specs=pl.BlockSpec(memory_space=pl.ANY),
        scratch_shapes=[pltpu.VMEM((tm, D), F32), pltpu.VMEM((group, D), F32),
                        pltpu.SemaphoreType.DMA(())],
    )
    return pl.pallas_call(
        functools.partial(_scatter_kernel, group=group),
        grid_spec=grid_spec,
        out_shape=jax.ShapeDtypeStruct((n_rows, D), F32),
        compiler_params=_params(("arbitrary",)),
        name=name,
    )(fills, h, gain, shift, scale, slots)


def _expert_kernel(be_ref, nu_ref, x_ref, wgu_ref, bgu_ref, wd_ref, bd_ref, y_ref):
    b = pl.program_id(0)

    @pl.when(b < nu_ref[0])
    def _():
        x = x_ref[...].astype(BF16)
        hgu = jnp.dot(x, wgu_ref[0], preferred_element_type=F32) + bgu_ref[0]
        even = (lax.broadcasted_iota(I32, (x.shape[0], LANES), 1) & 1) == 0
        parts = []
        for t in range(D_FF // LANES):
            pair = []
            for u in range(2):
                z = hgu[:, (2 * t + u) * LANES:(2 * t + u + 1) * LANES]
                nxt = pltpu.roll(z, LANES - 1, axis=1)
                glu = jnp.minimum(z, SWIGLU_LIMIT)
                lin = jnp.clip(nxt, -SWIGLU_LIMIT, SWIGLU_LIMIT)
                pair.append(glu * jax.nn.sigmoid(SWIGLU_ALPHA * glu) * (lin + 1.0))
            parts.append(jnp.where(even, pair[0], pltpu.roll(pair[1], 1, axis=1)).astype(BF16))
        act = jnp.concatenate(parts, axis=-1)
        y_ref[...] = jnp.dot(act, wd_ref[0], preferred_element_type=F32) + bd_ref[0]

    @pl.when(b >= nu_ref[0])
    def _():
        y_ref[...] = jnp.zeros(y_ref.shape, F32)


def _expert_call(xbuf, block_expert, n_used, wgu, bgu, wd, bd, *, group, name):
    n_rows, D = xbuf.shape
    nblk = n_rows // group
    row_map = lambda b, be, nu: (jnp.minimum(b, nu[0] - 1), 0)
    grid_spec = pltpu.PrefetchScalarGridSpec(
        num_scalar_prefetch=2,
        grid=(nblk,),
        in_specs=[pl.BlockSpec((group, D), row_map),
                  pl.BlockSpec((1, D, 2 * D_FF), lambda b, be, nu: (be[b], 0, 0)),
                  pl.BlockSpec((1, 1, 2 * D_FF), lambda b, be, nu: (be[b], 0, 0)),
                  pl.BlockSpec((1, D_FF, D), lambda b, be, nu: (be[b], 0, 0)),
                  pl.BlockSpec((1, 1, D), lambda b, be, nu: (be[b], 0, 0))],
        out_specs=pl.BlockSpec((group, D), lambda b, be, nu: (b, 0)),
    )
    return pl.pallas_call(
        _expert_kernel,
        grid_spec=grid_spec,
        out_shape=jax.ShapeDtypeStruct((n_rows, D), F32),
        compiler_params=_params(("arbitrary",)),
        name=name,
    )(block_expert, n_used, xbuf, wgu, bgu, wd, bd)


def _combine_kernel(y_ref, slot_ref, gate_ref, h_ref, g_ref, out_ref, buf_ref, sem):
    tm = h_ref.shape[0]

    def issue(r, carry):
        for k in range(TOP_K):
            src = slot_ref[0, 0, r * TOP_K + k]
            pltpu.make_async_copy(y_ref.at[pl.ds(src, 1)], buf_ref.at[k, pl.ds(r, 1)], sem).start()
        return carry

    lax.fori_loop(0, tm, issue, 0, unroll=8)

    def drain(r, carry):
        pltpu.make_async_copy(y_ref.at[pl.ds(0, 1)], buf_ref.at[0, pl.ds(0, 1)], sem).wait()
        return carry

    lax.fori_loop(0, tm * TOP_K, drain, 0, unroll=8)

    gate = gate_ref[...]
    ff = gate[:, 0:1] * buf_ref[0]
    for k in range(1, TOP_K):
        ff = ff + gate[:, k:k + 1] * buf_ref[k]
    out_ref[...] = h_ref[...] + g_ref[0] * ff


def _combine_call(ybuf, slots, gates, h, gate2, *, rows_per_batch, tm, name):
    T, D = h.shape
    ni = rows_per_batch // tm
    return pl.pallas_call(
        _combine_kernel,
        grid=(T // tm,),
        in_specs=[pl.BlockSpec(memory_space=pl.ANY),
                  pl.BlockSpec((1, 1, tm * TOP_K), lambda i: (i, 0, 0), memory_space=pltpu.SMEM),
                  pl.BlockSpec((tm, LANES), lambda i: (i, 0)),
                  pl.BlockSpec((tm, D), lambda i: (i, 0)),
                  pl.BlockSpec((1, 1, D), lambda i: (i // ni, 0, 0))],
        out_specs=pl.BlockSpec((tm, D), lambda i: (i, 0)),
        out_shape=jax.ShapeDtypeStruct((T, D), F32),
        scratch_shapes=[pltpu.VMEM((TOP_K, tm, D), F32), pltpu.SemaphoreType.DMA(())],
        compiler_params=_params(("arbitrary",)),
        name=name,
    )(ybuf, slots, gates, h, gate2)


def _moe(h, gain, shift, scale, gate2, w_router_pad, b_router_pad, wgu, bgu, wd, bd, *,
         rows_per_batch, tm, group, tag):
    T, D = h.shape
    A = T * TOP_K
    carry = jnp.zeros((1, LANES), F32)
    meta, gates, counts = _router_call(h, gain, shift, scale, w_router_pad, b_router_pad, carry,
                                       rows_per_batch=rows_per_batch, tm=tm, name="router_" + tag)
    counts = counts[0, :N_EXPERTS].astype(I32)
    padded = (counts + group - 1) // group * group
    ends = jnp.cumsum(padded)
    pstart = ends - padded
    nblk = -(-(A + N_EXPERTS * (group - 1)) // group)
    n_rows = nblk * group
    block_start = jnp.arange(nblk, dtype=I32) * group
    block_expert = jnp.minimum(
        jnp.sum((ends[None, :] <= block_start[:, None]).astype(I32), axis=1), N_EXPERTS - 1)
    n_used = jnp.maximum(ends[-1:] // group, 1).astype(I32)
    tails = jnp.where(padded > 0, ends - group, -1)
    spare = jnp.arange(A // group, nblk, dtype=I32)
    fills = jnp.concatenate([tails, jnp.where(spare >= n_used[0], spare * group, -1)]).astype(I32)
    idx = meta[:, :TOP_K]
    rank = meta[:, TOP_K:2 * TOP_K]
    onehot = idx[:, :, None] == jnp.arange(N_EXPERTS, dtype=I32)[None, None, :]
    slot = jnp.sum(jnp.where(onehot, pstart[None, None, :], 0), axis=-1) + rank
    slots = slot.reshape(T // tm, 1, tm * TOP_K).astype(I32)

    xbuf = _scatter_call(h, gain, shift, scale, slots, fills, rows_per_batch=rows_per_batch,
                         tm=tm, n_rows=n_rows, group=group, name="dispatch_" + tag)
    ybuf = _expert_call(xbuf, block_expert, n_used, wgu, bgu, wd, bd, group=group,
                        name="experts_" + tag)
    return _combine_call(ybuf, slots, gates, h, gate2, rows_per_batch=rows_per_batch, tm=tm,
                         name="combine_" + tag)


def _band_kernel(q_ref, k_ref, v_ref, bias_ref, o_ref, *, sub):
    qi = pl.program_id(2)
    tq = q_ref.shape[0]
    window = sub + BAND_PAST
    start = pl.multiple_of(qi * tq, tq)
    kall = k_ref[0, pl.ds(start, tq + BAND_PAST), :]
    vall = v_ref[0, pl.ds(start, tq + BAND_PAST), :]
    lane = lax.broadcasted_iota(I32, (sub, LANES), 1)

    def body(first_tile):
        for u in range(tq // sub):
            q = q_ref[u * sub:(u + 1) * sub, :]
            kw = kall[u * sub:u * sub + window]
            v_ext = _with_ones(vall[u * sub:u * sub + window])
            outs = []
            for a in range(2):
                qa = jnp.where((lane >= DB) == bool(a), q, jnp.zeros_like(q))
                s = _nt_dot(qa, kw) + bias_ref[a]
                if first_tile and u * sub < BAND_PAST:
                    col = lax.broadcasted_iota(I32, (sub, window), 1)
                    s = jnp.where(col >= BAND_PAST - u * sub, s, NEG_INF)
                m = jnp.max(s, axis=-1, keepdims=True)
                p = jnp.exp2(s - m).astype(BF16)
                pv = jnp.dot(p, v_ext, preferred_element_type=F32)
                outs.append(pv[:, :LANES] / pv[:, LANES:])
            o_ref[u * sub:(u + 1) * sub, :] = jnp.where(lane < DB, outs[0], outs[1])

    pl.when(qi == 0)(lambda: body(True))
    pl.when(qi > 0)(lambda: body(False))


def _band_call(q, kpad, vpad, bias, *, nb, seq, sub, tq=512):
    T = q.shape[0]
    nq = seq // tq
    window = sub + BAND_PAST
    npair = H_B // 2
    assert tq >= BAND_PAST
    return pl.pallas_call(
        functools.partial(_band_kernel, sub=sub),
        grid=(nb, npair, nq),
        in_specs=[pl.BlockSpec((tq, LANES), lambda b, j, i: (b * nq + i, j)),
                  pl.BlockSpec((1, BAND_PAST + seq, LANES), lambda b, j, i: (b, 0, j)),
                  pl.BlockSpec((1, BAND_PAST + seq, LANES), lambda b, j, i: (b, 0, j)),
                  pl.BlockSpec((2, sub, window), lambda b, j, i: (j, 0, 0))],
        out_specs=pl.BlockSpec((tq, LANES), lambda b, j, i: (b * nq + i, j)),
        out_shape=jax.ShapeDtypeStruct((T, D_MODEL), F32),
        compiler_params=_params(("arbitrary", "arbitrary", "arbitrary")),
        name="band_attn_prompt",
    )(q, kpad, vpad, bias)


def _band_dec_kernel(q_ref, ck_ref, cv_ref, kn_ref, vn_ref, bc_ref, bn_ref, o_ref):
    tq = q_ref.shape[0]
    lane = lax.broadcasted_iota(I32, (tq, LANES), 1)
    for j in range(H_B // 2):
        cs = slice(j * LANES, (j + 1) * LANES)
        q = q_ref[:, cs]
        kc = ck_ref[0, :, cs].astype(BF16)
        vc = cv_ref[0, :, cs].astype(BF16)
        kn = kn_ref[:, cs].astype(BF16)
        vn = vn_ref[:, cs].astype(BF16)
        outs = []
        for a in range(2):
            h = 2 * j + a
            qa = jnp.where((lane >= DB) == bool(a), q, jnp.zeros_like(q))
            sc = _nt_dot(qa, kc) + bc_ref[h]
            sn = _nt_dot(qa, kn) + bn_ref[h]
            m = jnp.maximum(jnp.max(sc, axis=-1, keepdims=True), jnp.max(sn, axis=-1, keepdims=True))
            pc = jnp.exp2(sc - m)
            pn = jnp.exp2(sn - m)
            l = jnp.sum(pc, axis=-1, keepdims=True) + jnp.sum(pn, axis=-1, keepdims=True)
            o = (jnp.dot(pc.astype(BF16), vc, preferred_element_type=F32)
                 + jnp.dot(pn.astype(BF16), vn, preferred_element_type=F32))
            outs.append(o / l)
        o_ref[:, cs] = jnp.where(lane < DB, outs[0], outs[1])


def _band_dec_call(q, cache_k, cache_v, k_new, v_new, bias_c, bias_n):
    nb, R, _ = cache_k.shape
    S = q.shape[0] // nb
    return pl.pallas_call(
        _band_dec_kernel,
        grid=(nb,),
        in_specs=[pl.BlockSpec((S, D_MODEL), lambda b: (b, 0)),
                  pl.BlockSpec((1, R, D_MODEL), lambda b: (b, 0, 0)),
                  pl.BlockSpec((1, R, D_MODEL), lambda b: (b, 0, 0)),
                  pl.BlockSpec((S, D_MODEL), lambda b: (b, 0)),
                  pl.BlockSpec((S, D_MODEL), lambda b: (b, 0)),
                  pl.BlockSpec(bias_c.shape, lambda b: (0, 0, 0)),
                  pl.BlockSpec(bias_n.shape, lambda b: (0, 0, 0))],
        out_specs=pl.BlockSpec((S, D_MODEL), lambda b: (b, 0)),
        out_shape=jax.ShapeDtypeStruct((nb * S, D_MODEL), F32),
        compiler_params=_params(("arbitrary",)),
        name="band_attn_sample",
    )(q, cache_k, cache_v, k_new, v_new, bias_c, bias_n)


def _rope_tables(past_len, seq):
    half = DA // 2
    inv = ROPE_THETA ** (-jnp.arange(half, dtype=F32) / half)
    pos = (past_len + jnp.arange(seq)).astype(F32)
    ang = pos[:, None] * inv[None, :]
    cos = jnp.cos(ang)
    sin = jnp.sin(ang)
    cos_t = jnp.tile(cos, (1, LANES // half))
    sin_t = jnp.tile(jnp.concatenate([-sin, sin], axis=-1), (1, LANES // DA))
    return cos_t, sin_t


def _band_bias(table, n_q, n_past, n_k, band_mask):
    dist = n_q - 1 + n_past - np.arange(n_q + n_k - 1)
    idx = np.clip(dist, -(CHUNK - 1), REL_MAX) + (CHUNK - 1)
    rev = table.astype(F32)[:, idx]
    bias = jnp.stack([rev[:, n_q - 1 - r:n_q - 1 - r + n_k] for r in range(n_q)], axis=1)
    if band_mask:
        cq = np.arange(n_q)[:, None] // CHUNK
        ck = np.arange(n_k)[None, :] // CHUNK
        inside = (ck >= cq) & (ck <= cq + BAND_PAST // CHUNK)
        bias = jnp.where(jnp.asarray(inside)[None], bias, NEG_INF)
    return bias


def _tile_gain(g):
    return jnp.tile(g.astype(F32), LANES // g.shape[0])[None, :]


def _permute_down(w):
    lead = w.shape[:-2]
    half = LANES // 2
    w = w.reshape(lead + (D_FF // LANES, 2, half, w.shape[-1]))
    w = jnp.swapaxes(w, -3, -2)
    return w.reshape(lead + (D_FF, w.shape[-1]))


def kernel(x_prompt, x_sample, c_prompt, c_sample, cache_a_k, cache_a_v, cache_b_k, cache_b_v, w_mod, b_mod, norm_attn, norm_ffn, w_qkv_a, q_norm_a, k_norm_a, lambda_a, subln_a, w_o_a, w_mod_kv, b_mod_kv, norm_kv, w_kv_b, k_norm_b, w_q_b, q_norm_b, rel_bias_b, w_o_b, w_router, b_router, w_gu, b_gu, w_down, b_down):
    D = D_MODEL
    nbp, seq_p, _ = x_prompt.shape
    nbs, seq_s, _ = x_sample.shape
    past_len = cache_a_k.shape[2]
    band_rows = cache_b_k.shape[1]
    depth = w_mod.shape[0]
    groups = (
        dict(tag="p", nb=nbp, seq=seq_p, tm=256, group=256, past=0),
        dict(tag="s", nb=nbs, seq=seq_s, tm=seq_s, group=128, past=past_len),
    )

    nrow = nbp + nbs
    rpad = -(-nrow // 8) * 8
    c_all = jnp.concatenate([c_prompt, c_sample, jnp.zeros((rpad - nrow, D), F32)], axis=0)
    mod = _mod_call(c_all, w_mod, b_mod[:, None, :])
    mod_kv = _mod_call(c_all, w_mod_kv[None], b_mod_kv[None, None, :])[0]

    def mod_vec(arr, g, k):
        lo = 0 if g["tag"] == "p" else nbp
        return arr[lo:lo + g["nb"], k * D:(k + 1) * D][:, None, :]

    w_qkv = w_qkv_a[0].astype(BF16)
    w_oa = w_o_a[0].astype(BF16)
    w_kv = w_kv_b.astype(BF16)
    w_qb = w_q_b[0].astype(BF16)
    w_ob = w_o_b[0].astype(BF16)
    gains_a = jnp.concatenate([_tile_gain(q_norm_a[0]), _tile_gain(k_norm_a[0]),
                               jnp.ones((1, LANES), F32)], axis=0)
    gains_kv = jnp.concatenate([_tile_gain(k_norm_b), jnp.ones((1, LANES), F32)], axis=0)
    gains_qb = _tile_gain(q_norm_b[0])
    sub_a = subln_a[0].astype(F32)[None, :]
    lam_init = 0.8 - 0.6 * math.exp(-0.3 * 0)
    wr_pad = jnp.pad(w_router, ((0, 0), (0, 0), (0, LANES - N_EXPERTS)))
    br_pad = jnp.pad(b_router, ((0, 0), (0, LANES - N_EXPERTS)))[:, None, :]
    wgu_b = w_gu.astype(BF16)
    bgu_d = b_gu[:, :, None, :]
    wd_b = _permute_down(w_down).astype(BF16)
    bd_d = b_down[:, :, None, :]

    xs = {"p": x_prompt.reshape(nbp * seq_p, D), "s": x_sample.reshape(nbs * seq_s, D)}
    outs = {}
    for g in groups:
        tag, nb, seq, tm = g["tag"], g["nb"], g["seq"], g["tm"]
        h = xs[tag]
        sel = functools.partial(mod_vec, g=g)
        cos_t, sin_t = _rope_tables(g["past"], seq)

        l = 0
        sh1, sc1, g1, sh2, sc2, g2 = [sel(mod[l], k=k) for k in range(6)]
        q, k32, kb, v32, vb = _proj_call(
            h, norm_attn[l][None, :], sh1, sc1, w_qkv, gains_a, cos_t, sin_t,
            rows_per_batch=seq, tm=tm,
            sections=((True, True, DA ** -0.5 * LOG2_E, False, True), (True, True, 1.0, True, True),
                      (False, False, 1.0, True, True)),
            name="qkv_proj_" + tag)
        if tag == "p":
            o = _flash_call(q, kb, vb, lambda_a[0], nb=nb, seq=seq, lam_init=lam_init)
        else:
            ck = cache_a_k[0].reshape(nb, past_len, D)
            cv = cache_a_v[0].reshape(nb, past_len, D)
            o = _decode_call(q, ck, cv, k32, v32, lambda_a[0], lam_init=lam_init)
        h = _out_call(o, sub_a, w_oa, h, g1, rows_per_batch=seq, tm=tm,
                      norm_scale=1.0 - lam_init, name="attn_out_a_" + tag)
        h = _moe(h, norm_ffn[l][None, :], sh2, sc2, g2, wr_pad[l], br_pad[l],
                 wgu_b[l], bgu_d[l], wd_b[l], bd_d[l],
                 rows_per_batch=seq, tm=tm, group=g["group"], tag="l0" + tag)
        a_k = k32.reshape(1, nb, seq, 2 * H_A, DA)
        a_v = v32.reshape(1, nb, seq, H_A, DV_A)

        shk, sck = [sel(mod_kv, k=k) for k in range(2)]
        l = 1
        sh1, sc1, g1, sh2, sc2, g2 = [sel(mod[l], k=k) for k in range(6)]
        kv_sections = ((True, False, 1.0, True, tag == "p"), (False, False, 1.0, True, tag == "p"))
        if tag == "p":
            kb32, kbp, vb32, vbp = _proj_call(
                h, norm_kv[None, :], shk, sck, w_kv, gains_kv, cos_t, sin_t,
                rows_per_batch=seq, tm=BAND_PAST, sections=kv_sections, pad=True,
                name="kv_proj_" + tag)
            new_b_k = kb32[:, -band_rows:].reshape(nb, band_rows, H_B, DB)
            new_b_v = vb32[:, -band_rows:].reshape(nb, band_rows, H_B, DB)
        else:
            kb32, vb32 = _proj_call(
                h, norm_kv[None, :], shk, sck, w_kv, gains_kv, cos_t, sin_t,
                rows_per_batch=seq, tm=tm, sections=kv_sections, name="kv_proj_" + tag)
            ckb = cache_b_k.reshape(nb, band_rows, D)
            cvb = cache_b_v.reshape(nb, band_rows, D)
            new_b_k = jnp.concatenate([ckb, kb32.reshape(nb, seq, D)], axis=1)[:, -band_rows:]
            new_b_v = jnp.concatenate([cvb, vb32.reshape(nb, seq, D)], axis=1)[:, -band_rows:]
            new_b_k = new_b_k.reshape(nb, band_rows, H_B, DB)
            new_b_v = new_b_v.reshape(nb, band_rows, H_B, DB)

        (qb,) = _proj_call(
            h, norm_attn[l][None, :], sh1, sc1, w_qb, gains_qb, cos_t, sin_t,
            rows_per_batch=seq, tm=tm, sections=((True, False, DB ** -0.5 * LOG2_E, False, True),),
            name="q_proj_b_" + tag)
        if tag == "p":
            sub = 2 * CHUNK
            bias = _band_bias(rel_bias_b[0], sub, BAND_PAST, sub + BAND_PAST, True) * LOG2_E
            o = _band_call(qb, kbp, vbp, bias, nb=nb, seq=seq, sub=sub)
        else:
            bias = _band_bias(rel_bias_b[0], seq, band_rows, band_rows + seq, False) * LOG2_E
            o = _band_dec_call(qb, ckb, cvb, kb32, vb32, bias[:, :, :band_rows], bias[:, :, band_rows:])
        h = _out_call(o, sub_a, w_ob, h, g1, rows_per_batch=seq, tm=tm, norm_scale=None,
                      name="attn_out_b_" + tag)
        h = _moe(h, norm_ffn[l][None, :], sh2, sc2, g2, wr_pad[l], br_pad[l],
                 wgu_b[l], bgu_d[l], wd_b[l], bd_d[l],
                 rows_per_batch=seq, tm=tm, group=g["group"], tag="l1" + tag)
        outs[tag] = (h.reshape(nb, seq, D), a_k, a_v, new_b_k, new_b_v)

    yp, akp, avp, bkp, bvp = outs["p"]
    ys, aks, avs, bks, bvs = outs["s"]
    return (yp, ys, akp, avp, bkp, bvp, aks, avs, bks, bvs)
```
